```python
import math
import jax, jax.numpy as jnp
from jax import lax
import numpy as np

D_MODEL = 2048
BATCH = 2
SEQ = 4096
DEPTH = 2

CHUNK = 64
QBLOCK = 128
ROPE_THETA = 10000.0
EPS = 1e-6
NEG_INF = -1e30

MLA_HEADS = 8
MLA_Q_RANK = 512
MLA_KV_RANK = 256
MLA_NOPE = 128
MLA_ROPE = 64
MLA_V = 128
MLA_OUT = MLA_HEADS * MLA_V

DIFF_HEADS = 8
DIFF_HEAD_DIM = 64
DIFF_QK = DIFF_HEADS * 2 * DIFF_HEAD_DIM
DIFF_OUT = DIFF_HEADS * 2 * DIFF_HEAD_DIM

IN_SPLITS = (MLA_Q_RANK, MLA_KV_RANK, MLA_ROPE, DIFF_QK, DIFF_QK, DIFF_OUT, D_MODEL, D_MODEL)
D_IN = int(sum(IN_SPLITS))
IN_OFFSETS = tuple(int(o) for o in np.cumsum(IN_SPLITS)[:-1])

D_FF_DENSE = 5632
N_EXPERTS = 8
TOP_K = 2
D_FF_EXPERT = 7168
N_DENSE = (DEPTH + 1) // 2
N_MOE = DEPTH // 2

kernel_name = "hybrid_mla_diffattn_gated_moe_block"


def _rms_norm(x, g):
    xf = x.astype(jnp.float32)
    y = xf * lax.rsqrt(jnp.mean(xf * xf, axis=-1, keepdims=True) + EPS)
    return (y * g.astype(jnp.float32)).astype(x.dtype)


def _rope(x, positions):
    d = x.shape[-1]
    inv_freq = ROPE_THETA ** (-jnp.arange(0, d, 2, dtype=jnp.float32) / d)
    ang = positions.astype(jnp.float32)[..., None] * inv_freq
    ang = ang.reshape(ang.shape[:2] + (1,) * (x.ndim - 3) + (d // 2,))
    cos, sin = jnp.cos(ang), jnp.sin(ang)
    xf = x.astype(jnp.float32)
    x1, x2 = xf[..., : d // 2], xf[..., d // 2:]
    return jnp.concatenate([x1 * cos - x2 * sin, x2 * cos + x1 * sin], axis=-1).astype(x.dtype)


def _chunk_causal_attention(q, k, v, scale, lam=None):
    B, M, H, S, d = q.shape
    nb = S // QBLOCK
    qb = jnp.moveaxis(q.reshape(B, M, H, nb, QBLOCK, d), 3, 0)
    starts = jnp.arange(nb, dtype=jnp.int32) * QBLOCK
    key_chunk = jnp.arange(S, dtype=jnp.int32) // CHUNK

    def block(args):
        qblk, start = args
        s = jnp.einsum('bmhqd,bmhkd->bmhqk', qblk, k, preferred_element_type=jnp.float32) * scale
        q_chunk = (start + jnp.arange(QBLOCK, dtype=jnp.int32)) // CHUNK
        mask = key_chunk[None, :] <= q_chunk[:, None]
        p = jax.nn.softmax(jnp.where(mask, s, NEG_INF), axis=-1)
        w = p[:, 0] if M == 1 else p[:, 0] - lam * p[:, 1]
        return jnp.einsum('bhqk,bhkd->bhqd', w.astype(v.dtype), v)

    out = lax.map(block, (qb, starts))
    return jnp.moveaxis(out, 0, 2).reshape(B, H, S, v.shape[-1])


def _token_mixer(h, positions, layer, w_in, q_norm_g, kv_norm_g, w_q_up, w_kv_up,
                 diff_lambda, diff_subln_g, w_branch_a, w_branch_b, b_gate, w_out):
    B, S, _ = h.shape
    proj = h @ w_in
    q_lat, kv_lat, k_rope, dq, dk, dv, ga, gb = jnp.split(proj, IN_OFFSETS, axis=-1)

    q = (_rms_norm(q_lat, q_norm_g) @ w_q_up).reshape(B, S, MLA_HEADS, MLA_NOPE + MLA_ROPE)
    q_nope, q_pe = q[..., :MLA_NOPE], _rope(q[..., MLA_NOPE:], positions)
    kv = (_rms_norm(kv_lat, kv_norm_g) @ w_kv_up).reshape(B, S, MLA_HEADS, MLA_NOPE + MLA_V)
    k_nope, v_a = kv[..., :MLA_NOPE], kv[..., MLA_NOPE:]
    k_pe = jnp.broadcast_to(_rope(k_rope, positions)[:, :, None, :], (B, S, MLA_HEADS, MLA_ROPE))
    qa = jnp.concatenate([q_nope, q_pe], axis=-1).transpose(0, 2, 1, 3)[:, None]
    ka = jnp.concatenate([k_nope, k_pe], axis=-1).transpose(0, 2, 1, 3)[:, None]
    oa = _chunk_causal_attention(qa, ka, v_a.transpose(0, 2, 1, 3), (MLA_NOPE + MLA_ROPE) ** -0.5)
    ya = oa.transpose(0, 2, 1, 3).reshape(B, S, MLA_OUT)

    lam_init = 0.8 - 0.6 * math.exp(-0.3 * layer)
    lp = diff_lambda.astype(jnp.float32)
    lam = jnp.exp(jnp.sum(lp[0] * lp[1])) - jnp.exp(jnp.sum(lp[2] * lp[3])) + lam_init
    qd = _rope(dq.reshape(B, S, DIFF_HEADS, 2, DIFF_HEAD_DIM), positions).transpose(0, 3, 2, 1, 4)
    kd = _rope(dk.reshape(B, S, DIFF_HEADS, 2, DIFF_HEAD_DIM), positions).transpose(0, 3, 2, 1, 4)
    vd = dv.reshape(B, S, DIFF_HEADS, 2 * DIFF_HEAD_DIM).transpose(0, 2, 1, 3)
    od = _chunk_causal_attention(qd, kd, vd, DIFF_HEAD_DIM ** -0.5, lam)
    od = _rms_norm(od, diff_subln_g) * (1.0 - lam_init)
    yb = od.transpose(0, 2, 1, 3).reshape(B, S, DIFF_OUT)

    merged = (jax.nn.sigmoid(ga + b_gate[0]) * (ya @ w_branch_a)
              + jax.nn.sigmoid(gb + b_gate[1]) * (yb @ w_branch_b))
    return merged @ w_out


def _swiglu(h, w_gu, w_down):
    g, u = jnp.split(h @ w_gu, 2, axis=-1)
    return (jax.nn.silu(g) * u) @ w_down


def _moe(h, router_w, moe_w_gu, moe_w_down):
    logits = (h @ router_w).astype(jnp.float32)
    top_v, top_i = lax.top_k(logits, TOP_K)
    top_w = jax.nn.softmax(top_v, axis=-1)
    gates = jnp.sum(jax.nn.one_hot(top_i, N_EXPERTS, dtype=jnp.float32) * top_w[..., None], axis=-2)
    out = jnp.zeros_like(h)
    for e in range(N_EXPERTS):
        out = out + gates[..., e:e + 1].astype(h.dtype) * _swiglu(h, moe_w_gu[e], moe_w_down[e])
    return out


def _adaln_pre(x, cs, w, b, g_pre):
    mod = (cs @ w + b)[:, None, :]
    shift, scale, gate = jnp.split(mod, 3, axis=-1)
    return _rms_norm(x, g_pre) * (1.0 + scale) + shift, gate


def setup_inputs(seed: int = 0) -> dict:
    key = jax.random.key(seed)
    ks = list(jax.random.split(key, 24))
    L = DEPTH
    nrm = lambda k, shape, s: jax.random.normal(k, shape, jnp.float32) * s
    offset = jax.random.randint(ks[2], (BATCH,), 0, 64, dtype=jnp.int32) * CHUNK
    positions = (offset[:, None] + jnp.arange(SEQ, dtype=jnp.int32)[None, :]).astype(jnp.int32)
    return {
        "x": nrm(ks[0], (BATCH, SEQ, D_MODEL), 1.0),
        "c": nrm(ks[1], (BATCH, D_MODEL), 1.0),
        "positions": positions,
        "ada_w": nrm(ks[3], (L, 2, D_MODEL, 3 * D_MODEL), D_MODEL ** -0.5),
        "ada_b": nrm(ks[4], (L, 2, 3 * D_MODEL), 0.01),
        "norm_g": 1.0 + nrm(ks[5], (L, 4, D_MODEL), 0.05),
        "w_in": nrm(ks[6], (L, D_MODEL, D_IN), D_MODEL ** -0.5),
        "q_norm_g": 1.0 + nrm(ks[7], (L, MLA_Q_RANK), 0.05),
        "kv_norm_g": 1.0 + nrm(ks[8], (L, MLA_KV_RANK), 0.05),
        "w_q_up": nrm(ks[9], (L, MLA_Q_RANK, MLA_HEADS * (MLA_NOPE + MLA_ROPE)), MLA_Q_RANK ** -0.5),
        "w_kv_up": nrm(ks[10], (L, MLA_KV_RANK, MLA_HEADS * (MLA_NOPE + MLA_V)), MLA_KV_RANK ** -0.5),
        "diff_lambda": nrm(ks[11], (L, 4, DIFF_HEAD_DIM), 0.1),
        "diff_subln_g": 1.0 + nrm(ks[12], (L, 2 * DIFF_HEAD_DIM), 0.05),
        "w_branch_a": nrm(ks[13], (L, MLA_OUT, D_MODEL), MLA_OUT ** -0.5),
        "w_branch_b": nrm(ks[14], (L, DIFF_OUT, D_MODEL), DIFF_OUT ** -0.5),
        "b_gate": nrm(ks[15], (L, 2, D_MODEL), 0.01),
        "w_out": nrm(ks[16], (L, D_MODEL, D_MODEL), D_MODEL ** -0.5),
        "dense_w_gu": nrm(ks[17], (N_DENSE, D_MODEL, 2 * D_FF_DENSE), D_MODEL ** -0.5),
        "dense_w_down": nrm(ks[18], (N_DENSE, D_FF_DENSE, D_MODEL), D_FF_DENSE ** -0.5),
        "router_w": nrm(ks[19], (N_MOE, D_MODEL, N_EXPERTS), D_MODEL ** -0.5),
        "moe_w_gu": nrm(ks[20], (N_MOE, N_EXPERTS, D_MODEL, 2 * D_FF_EXPERT), D_MODEL ** -0.5),
        "moe_w_down": nrm(ks[21], (N_MOE, N_EXPERTS, D_FF_EXPERT, D_MODEL), D_FF_EXPERT ** -0.5),
    }


def reference(x, c, positions, ada_w, ada_b, norm_g, w_in, q_norm_g, kv_norm_g, w_q_up, w_kv_up,
              diff_lambda, diff_subln_g, w_branch_a, w_branch_b, b_gate, w_out,
              dense_w_gu, dense_w_down, router_w, moe_w_gu, moe_w_down):
    cs = jax.nn.silu(c)
    for l in range(DEPTH):
        h, gate = _adaln_pre(x, cs, ada_w[l, 0], ada_b[l, 0], norm_g[l, 0])
        y = _token_mixer(h, positions, l, w_in[l], q_norm_g[l], kv_norm_g[l], w_q_up[l], w_kv_up[l],
                         diff_lambda[l], diff_subln_g[l], w_branch_a[l], w_branch_b[l], b_gate[l], w_out[l])
        x = x + gate * _rms_norm(y, norm_g[l, 1])
        h, gate = _adaln_pre(x, cs, ada_w[l, 1], ada_b[l, 1], norm_g[l, 2])
        if l % 2 == 0:
            y = _swiglu(h, dense_w_gu[l // 2], dense_w_down[l // 2])
        else:
            y = _moe(h, router_w[l // 2], moe_w_gu[l // 2], moe_w_down[l // 2])
        x = x + gate * _rms_norm(y, norm_g[l, 3])
    return x
```

```python
import functools
import math

import jax
import jax.numpy as jnp
import numpy as np
from jax import lax
from jax.experimental import pallas as pl
from jax.experimental.pallas import tpu as pltpu

F32 = jnp.float32
BF16 = jnp.bfloat16

D_MODEL = 2048
BATCH = 2
SEQ = 4096
TOKENS = BATCH * SEQ
DEPTH = 2
CHUNK = 64
ROPE_THETA = 10000.0
EPS = 1e-6
NEG_INF = -1e30

MLA_HEADS = 8
MLA_Q_RANK = 512
MLA_KV_RANK = 256
MLA_NOPE = 128
MLA_ROPE = 64
MLA_V = 128
MLA_QK_PAD = 256

DIFF_HEADS = 8
DIFF_HEAD_DIM = 64

D_FF_DENSE = 5632
N_EXPERTS = 8
TOP_K = 2
D_FF_EXPERT = 7168

D_IN_PAD = 8192

LANES = 128
V7X_VMEM_LIMIT = 56 * 1024 * 1024

TM_NORM = 512
TM_IN = 1024
TN_IN = 1024
TM_UP = 512
TQ = 512
TM_MERGE = 256
TM_DENSE = 512
TN_DENSE_GU = 512
TN_DENSE_DOWN = 512
TM_MOE = 256
TN_MOE_GU = 1024
TN_MOE_DOWN = 512
N_SORT_PAD = TOP_K * TOKENS + N_EXPERTS * TM_MOE
TM_COMBINE = 256


def _cparams(sem, vmem=None):
    return pltpu.CompilerParams(dimension_semantics=sem, vmem_limit_bytes=vmem)


def _nt_dot(a, b):
    return lax.dot_general(a, b, (((1,), (1,)), ((), ())), preferred_element_type=F32)


def _rms(y):
    return y * lax.rsqrt(jnp.mean(y * y, axis=-1, keepdims=True) + EPS)


def _rope_lanes(blk, cos, sin_signed, first_half):
    rot = jnp.where(first_half, pltpu.roll(blk, 96, 1), pltpu.roll(blk, 32, 1))
    return blk * cos + rot * sin_signed


def _first_half_mask(rows):
    lane = lax.broadcasted_iota(jnp.int32, (rows, LANES), 1)
    return (lane % 64) < 32


def _rope_tables_kernel(pos_ref, freq_ref, sign_ref, cos_ref, sin_ref):
    ang = pos_ref[...].astype(F32) * freq_ref[...]
    cos_ref[...] = jnp.cos(ang)
    sin_ref[...] = jnp.sin(ang) * sign_ref[...]


def _rope_tables(positions):
    half = MLA_ROPE // 2
    inv_freq = ROPE_THETA ** (-jnp.arange(0, MLA_ROPE, 2, dtype=F32) / MLA_ROPE)
    freq = jnp.tile(inv_freq, LANES // half).reshape(1, LANES)
    sign = jnp.tile(jnp.concatenate([-jnp.ones((half,), F32), jnp.ones((half,), F32)]), LANES // MLA_ROPE)
    sign = sign.reshape(1, LANES)
    pos = positions.reshape(TOKENS, 1)
    tm = 1024
    return pl.pallas_call(
        _rope_tables_kernel,
        grid=(TOKENS // tm,),
        in_specs=[pl.BlockSpec((tm, 1), lambda i: (i, 0)),
                  pl.BlockSpec((1, LANES), lambda i: (0, 0)),
                  pl.BlockSpec((1, LANES), lambda i: (0, 0))],
        out_specs=[pl.BlockSpec((tm, LANES), lambda i: (i, 0)),
                   pl.BlockSpec((tm, LANES), lambda i: (i, 0))],
        out_shape=[jax.ShapeDtypeStruct((TOKENS, LANES), F32)] * 2,
        compiler_params=_cparams(("parallel",)),
        name="rope_tables",
    )(pos, freq, sign)


def _adaln_kernel(c_ref, w_ref, b_ref, o_ref):
    c = c_ref[...]
    cs = (c * jax.nn.sigmoid(c)).astype(BF16)
    o_ref[0] = jnp.dot(cs, w_ref[0].astype(BF16), preferred_element_type=F32) + b_ref[0]


def _adaln(c, ada_w, ada_b):
    n_mod = DEPTH * 2
    w = ada_w.reshape(n_mod, D_MODEL, 3 * D_MODEL)
    b = ada_b.reshape(n_mod, 1, 3 * D_MODEL)
    c8 = jnp.zeros((8, D_MODEL), F32).at[:BATCH].set(c)
    tn = 1024
    out = pl.pallas_call(
        _adaln_kernel,
        grid=(n_mod, 3 * D_MODEL // tn),
        in_specs=[pl.BlockSpec((8, D_MODEL), lambda m, n: (0, 0)),
                  pl.BlockSpec((1, D_MODEL, tn), lambda m, n: (m, 0, n)),
                  pl.BlockSpec((1, 1, tn), lambda m, n: (m, 0, n))],
        out_specs=pl.BlockSpec((1, 8, tn), lambda m, n: (m, 0, n)),
        out_shape=jax.ShapeDtypeStruct((n_mod, 8, 3 * D_MODEL), F32),
        compiler_params=_cparams(("parallel", "parallel"), V7X_VMEM_LIMIT),
        name="adaln",
    )(c8, w, b)
    mod = out[:, :BATCH, :].reshape(DEPTH, 2, BATCH, 1, 3 * D_MODEL)
    return mod


def _norm_mod_kernel(x_ref, g_ref, sc_ref, sh_ref, h_ref):
    h = _rms(x_ref[...]) * g_ref[...] * (1.0 + sc_ref[0]) + sh_ref[0]
    h_ref[...] = h.astype(h_ref.dtype)


def _norm_mod_route_kernel(x_ref, g_ref, sc_ref, sh_ref, rwh_ref, rwl_ref, h_ref, idx_ref, wt_ref):
    h = _rms(x_ref[...]) * g_ref[...] * (1.0 + sc_ref[0]) + sh_ref[0]
    h_ref[...] = h
    h_hi = h.astype(BF16)
    h_lo = (h - h_hi.astype(F32)).astype(BF16)
    logits = (jnp.dot(h_hi, rwh_ref[...], preferred_element_type=F32)
              + jnp.dot(h_hi, rwl_ref[...], preferred_element_type=F32)
              + jnp.dot(h_lo, rwh_ref[...], preferred_element_type=F32))
    lt = logits.T[:N_EXPERTS, :]
    eid = lax.broadcasted_iota(jnp.int32, lt.shape, 0)
    m1 = jnp.max(lt, axis=0, keepdims=True)
    i1 = jnp.min(jnp.where(lt == m1, eid, N_EXPERTS), axis=0, keepdims=True)
    lt2 = jnp.where(eid == i1, -jnp.inf, lt)
    m2 = jnp.max(lt2, axis=0, keepdims=True)
    i2 = jnp.min(jnp.where(lt2 == m2, eid, N_EXPERTS), axis=0, keepdims=True)
    e2 = jnp.exp(m2 - m1)
    w1 = 1.0 / (1.0 + e2)
    idx_ref[0:1, :] = i1
    idx_ref[1:2, :] = i2
    wt_ref[0:1, :] = w1
    wt_ref[1:2, :] = e2 * w1


def _mod_specs(tm):
    per_b = SEQ // tm
    return [pl.BlockSpec((tm, D_MODEL), lambda i: (i, 0)),
            pl.BlockSpec((1, D_MODEL), lambda i: (0, 0)),
            pl.BlockSpec((1, 1, D_MODEL), lambda i: (i // per_b, 0, 0)),
            pl.BlockSpec((1, 1, D_MODEL), lambda i: (i // per_b, 0, 0))]


def _norm_mod(x, g, scale, shift):
    tm = TM_NORM
    return pl.pallas_call(
        _norm_mod_kernel,
        grid=(TOKENS // tm,),
        in_specs=_mod_specs(tm),
        out_specs=pl.BlockSpec((tm, D_MODEL), lambda i: (i, 0)),
        out_shape=jax.ShapeDtypeStruct((TOKENS, D_MODEL), BF16),
        compiler_params=_cparams(("parallel",)),
        name="norm_mod",
    )(x, g.reshape(1, D_MODEL), scale, shift)


def _norm_mod_route(x, g, scale, shift, router_w):
    tm = TM_NORM
    rw = jnp.zeros((D_MODEL, LANES), F32).at[:, :N_EXPERTS].set(router_w)
    rw_hi = rw.astype(BF16)
    rw_lo = (rw - rw_hi.astype(F32)).astype(BF16)
    return pl.pallas_call(
        _norm_mod_route_kernel,
        grid=(TOKENS // tm,),
        in_specs=_mod_specs(tm) + [pl.BlockSpec((D_MODEL, LANES), lambda i: (0, 0)),
                                   pl.BlockSpec((D_MODEL, LANES), lambda i: (0, 0))],
        out_specs=[pl.BlockSpec((tm, D_MODEL), lambda i: (i, 0)),
                   pl.BlockSpec((TOP_K, tm), lambda i: (0, i)),
                   pl.BlockSpec((TOP_K, tm), lambda i: (0, i))],
        out_shape=[jax.ShapeDtypeStruct((TOKENS, D_MODEL), F32),
                   jax.ShapeDtypeStruct((TOP_K, TOKENS), jnp.int32),
                   jax.ShapeDtypeStruct((TOP_K, TOKENS), F32)],
        compiler_params=_cparams(("parallel",)),
        name="norm_mod_route",
    )(x, g.reshape(1, D_MODEL), scale, shift, rw_hi, rw_lo)


def _in_proj_kernel(h_ref, w_ref, cos_ref, sin_ref, o_ref):
    n = pl.program_id(0)
    acc = jnp.dot(h_ref[...], w_ref[...], preferred_element_type=F32)
    tm, tn = acc.shape

    def rope_cols(lo, hi, mult):
        fh = _first_half_mask(tm)
        cos, sin = cos_ref[...], sin_ref[...]
        for c in range(lo // LANES, hi // LANES):
            blk = acc[:, c * LANES:(c + 1) * LANES]
            o_ref[:, c * LANES:(c + 1) * LANES] = (_rope_lanes(blk, cos, sin, fh) * mult).astype(o_ref.dtype)

    @pl.when(n == 0)
    def _():
        lat = MLA_Q_RANK + MLA_KV_RANK
        o_ref[:, :lat] = acc[:, :lat].astype(o_ref.dtype)
        rope_cols(lat, tn, 1.0)

    @pl.when(n == 1)
    def _():
        rope_cols(0, tn, DIFF_HEAD_DIM ** -0.5)

    @pl.when(n == 2)
    def _():
        rope_cols(0, tn, 1.0)

    @pl.when(n >= 3)
    def _():
        o_ref[...] = acc.astype(o_ref.dtype)


def _in_proj(h, w_in_p, cos, sin):
    tm, tn = TM_IN, TN_IN
    return pl.pallas_call(
        _in_proj_kernel,
        grid=(D_IN_PAD // tn, TOKENS // tm),
        in_specs=[pl.BlockSpec((tm, D_MODEL), lambda n, m: (m, 0)),
                  pl.BlockSpec((D_MODEL, tn), lambda n, m: (0, n)),
                  pl.BlockSpec((tm, LANES), lambda n, m: (m, 0)),
                  pl.BlockSpec((tm, LANES), lambda n, m: (m, 0))],
        out_specs=pl.BlockSpec((tm, tn), lambda n, m: (m, n)),
        out_shape=jax.ShapeDtypeStruct((TOKENS, D_IN_PAD), BF16),
        compiler_params=_cparams(("parallel", "parallel"), V7X_VMEM_LIMIT),
        name="in_proj",
    )(h, w_in_p, cos, sin)


def _mla_up_kernel(lat_ref, qg_ref, kvg_ref, wq_ref, wkv_ref, cos_ref, sin_ref, q_ref, k_ref, v_ref):
    tm = lat_ref.shape[0]
    scale = (MLA_NOPE + MLA_ROPE) ** -0.5
    q_lat = lat_ref[:, :MLA_Q_RANK].astype(F32)
    kv_lat = lat_ref[:, MLA_Q_RANK:MLA_Q_RANK + MLA_KV_RANK].astype(F32)
    k_pe = lat_ref[:, MLA_Q_RANK + MLA_KV_RANK:MLA_Q_RANK + MLA_KV_RANK + LANES]
    qn = (_rms(q_lat) * qg_ref[...]).astype(BF16)
    kvn = (_rms(kv_lat) * kvg_ref[...]).astype(BF16)
    q = jnp.dot(qn, wq_ref[...], preferred_element_type=F32)
    kv = jnp.dot(kvn, wkv_ref[...], preferred_element_type=F32)
    fh = _first_half_mask(tm)
    cos, sin = cos_ref[...], sin_ref[...]
    for h in range(MLA_HEADS):
        c0 = h * MLA_QK_PAD
        q_ref[:, c0:c0 + LANES] = (q[:, c0:c0 + LANES] * scale).astype(BF16)
        q_pe = _rope_lanes(q[:, c0 + LANES:c0 + 2 * LANES], cos, sin, fh)
        q_ref[:, c0 + LANES:c0 + 2 * LANES] = (q_pe * scale).astype(BF16)
        k_ref[:, c0:c0 + LANES] = kv[:, h * LANES:(h + 1) * LANES].astype(BF16)
        k_ref[:, c0 + LANES:c0 + 2 * LANES] = k_pe
    v_ref[...] = kv[:, MLA_HEADS * MLA_NOPE:].astype(BF16)


def _mla_up(proj, q_norm_g, kv_norm_g, wq_p, wkv_p, cos, sin):
    tm = TM_UP
    lat_w = 1024
    hq = MLA_HEADS * MLA_QK_PAD
    hv = MLA_HEADS * MLA_V
    const = lambda i: (0, 0)
    return pl.pallas_call(
        _mla_up_kernel,
        grid=(TOKENS // tm,),
        in_specs=[pl.BlockSpec((tm, lat_w), lambda i: (i, 0)),
                  pl.BlockSpec((1, MLA_Q_RANK), const),
                  pl.BlockSpec((1, MLA_KV_RANK), const),
                  pl.BlockSpec((MLA_Q_RANK, hq), const),
                  pl.BlockSpec((MLA_KV_RANK, 2 * hv), const),
                  pl.BlockSpec((tm, LANES), lambda i: (i, 0)),
                  pl.BlockSpec((tm, LANES), lambda i: (i, 0))],
        out_specs=[pl.BlockSpec((tm, hq), lambda i: (i, 0)),
                   pl.BlockSpec((tm, hq), lambda i: (i, 0)),
                   pl.BlockSpec((tm, hv), lambda i: (i, 0))],
        out_shape=[jax.ShapeDtypeStruct((TOKENS, hq), BF16),
                   jax.ShapeDtypeStruct((TOKENS, hq), BF16),
                   jax.ShapeDtypeStruct((TOKENS, hv), BF16)],
        compiler_params=_cparams(("parallel",), V7X_VMEM_LIMIT),
        name="mla_up",
    )(proj, q_norm_g.reshape(1, -1), kv_norm_g.reshape(1, -1), wq_p, wkv_p, cos, sin)


def _chunk_mask(tq):
    qc = lax.broadcasted_iota(jnp.int32, (tq, tq), 0) // CHUNK
    kc = lax.broadcasted_iota(jnp.int32, (tq, tq), 1) // CHUNK
    return kc <= qc


def _online_softmax_step(s, v, m_s, l_s, acc_s):
    m_prev = m_s[...]
    m_new = jnp.maximum(m_prev, jnp.max(s, axis=1, keepdims=True))
    alpha = jnp.exp(m_prev - m_new)
    p = jnp.exp(s - m_new)
    l_s[...] = alpha * l_s[...] + jnp.sum(p, axis=1, keepdims=True)
    acc_s[...] = alpha * acc_s[...] + jnp.dot(p.astype(BF16), v, preferred_element_type=F32)
    m_s[...] = m_new


def _mla_attn_kernel(q_ref, k_ref, v_ref, o_ref, m_s, l_s, acc_s):
    tq = q_ref.shape[0]
    qi = pl.program_id(2)
    q = q_ref[...]
    m_s[...] = jnp.full(m_s.shape, -jnp.inf, F32)
    l_s[...] = jnp.zeros(l_s.shape, F32)
    acc_s[...] = jnp.zeros(acc_s.shape, F32)

    def tile(j, masked):
        rows = pl.ds(pl.multiple_of(j * tq, tq), tq)
        s = _nt_dot(q, k_ref[rows, :])
        if masked:
            s = jnp.where(_chunk_mask(tq), s, NEG_INF)
        _online_softmax_step(s, v_ref[rows, :], m_s, l_s, acc_s)

    def body(j, carry):
        tile(j, False)
        return carry

    lax.fori_loop(0, qi, body, 0)
    tile(qi, True)
    o_ref[...] = (acc_s[...] * (1.0 / l_s[...])).astype(o_ref.dtype)


def _mla_attn(qa, ka, va):
    tq = TQ
    nq = SEQ // tq
    return pl.pallas_call(
        _mla_attn_kernel,
        grid=(BATCH, MLA_HEADS, nq),
        in_specs=[pl.BlockSpec((tq, MLA_QK_PAD), lambda b, h, i: (b * nq + i, h)),
                  pl.BlockSpec((SEQ, MLA_QK_PAD), lambda b, h, i: (b, h)),
                  pl.BlockSpec((SEQ, MLA_V), lambda b, h, i: (b, h))],
        out_specs=pl.BlockSpec((tq, MLA_V), lambda b, h, i: (b * nq + i, h)),
        out_shape=jax.ShapeDtypeStruct((TOKENS, MLA_HEADS * MLA_V), BF16),
        scratch_shapes=[pltpu.VMEM((tq, 1), F32), pltpu.VMEM((tq, 1), F32), pltpu.VMEM((tq, MLA_V), F32)],
        compiler_params=_cparams(("parallel", "parallel", "parallel"), V7X_VMEM_LIMIT),
        name="mla_attn",
    )(qa, ka, va)


def _diff_attn_kernel(q_ref, k_ref, v_ref, lam_ref, g_ref, o_ref, m1, l1, a1, m2, l2, a2, *, lam_init):
    tq = q_ref.shape[0]
    qi = pl.program_id(2)
    q = q_ref[...]
    lane = lax.broadcasted_iota(jnp.int32, q.shape, 1)
    zero = jnp.zeros_like(q)
    q1 = jnp.where(lane < DIFF_HEAD_DIM, q, zero)
    q2 = jnp.where(lane >= DIFF_HEAD_DIM, q, zero)
    for m_s, l_s, a_s in ((m1, l1, a1), (m2, l2, a2)):
        m_s[...] = jnp.full(m_s.shape, -jnp.inf, F32)
        l_s[...] = jnp.zeros(l_s.shape, F32)
        a_s[...] = jnp.zeros(a_s.shape, F32)

    def tile(j, masked):
        rows = pl.ds(pl.multiple_of(j * tq, tq), tq)
        k = k_ref[rows, :]
        v = v_ref[rows, :]
        s1 = _nt_dot(q1, k)
        s2 = _nt_dot(q2, k)
        if masked:
            mask = _chunk_mask(tq)
            s1 = jnp.where(mask, s1, NEG_INF)
            s2 = jnp.where(mask, s2, NEG_INF)
        _online_softmax_step(s1, v, m1, l1, a1)
        _online_softmax_step(s2, v, m2, l2, a2)

    def body(j, carry):
        tile(j, False)
        return carry

    lax.fori_loop(0, qi, body, 0)
    tile(qi, True)
    lp = lam_ref[...]
    lam = (jnp.exp(jnp.sum(lp[0:1] * lp[1:2], axis=1, keepdims=True))
           - jnp.exp(jnp.sum(lp[2:3] * lp[3:4], axis=1, keepdims=True)) + lam_init)
    od = a1[...] * (1.0 / l1[...]) - lam * (a2[...] * (1.0 / l2[...]))
    o_ref[...] = (_rms(od) * g_ref[...] * (1.0 - lam_init)).astype(o_ref.dtype)


def _diff_attn(proj, diff_lambda, subln_g, layer):
    tq = TQ
    nq = SEQ // tq
    hd = 2 * DIFF_HEAD_DIM
    q0, k0, v0 = 1024 // hd, 2048 // hd, 3072 // hd
    lam_init = 0.8 - 0.6 * math.exp(-0.3 * layer)
    return pl.pallas_call(
        functools.partial(_diff_attn_kernel, lam_init=lam_init),
        grid=(BATCH, DIFF_HEADS, nq),
        in_specs=[pl.BlockSpec((tq, hd), lambda b, h, i: (b * nq + i, q0 + h)),
                  pl.BlockSpec((SEQ, hd), lambda b, h, i: (b, k0 + h)),
                  pl.BlockSpec((SEQ, hd), lambda b, h, i: (b, v0 + h)),
                  pl.BlockSpec((4, DIFF_HEAD_DIM), lambda b, h, i: (0, 0)),
                  pl.BlockSpec((1, hd), lambda b, h, i: (0, 0))],
        out_specs=pl.BlockSpec((tq, hd), lambda b, h, i: (b * nq + i, h)),
        out_shape=jax.ShapeDtypeStruct((TOKENS, DIFF_HEADS * hd), BF16),
        scratch_shapes=[pltpu.VMEM((tq, 1), F32), pltpu.VMEM((tq, 1), F32), pltpu.VMEM((tq, hd), F32)] * 2,
        compiler_params=_cparams(("parallel", "parallel", "parallel"), V7X_VMEM_LIMIT),
        name="diff_attn",
    )(proj, proj, proj, diff_lambda, subln_g.reshape(1, hd))


def _post_residual(x, y, gate, g):
    return x + gate * (_rms(y) * g)


def _merge_out_kernel(ya_ref, yb_ref, ga_ref, gb_ref, bg_ref, wa_ref, wb_ref, wo_ref, x_ref, gate_ref, g_ref,
                      o_ref):
    a = jnp.dot(ya_ref[...], wa_ref[...], preferred_element_type=F32)
    b = jnp.dot(yb_ref[...], wb_ref[...], preferred_element_type=F32)
    bg = bg_ref[...]
    merged = (jax.nn.sigmoid(ga_ref[...].astype(F32) + bg[0:1]) * a
              + jax.nn.sigmoid(gb_ref[...].astype(F32) + bg[1:2]) * b)
    y = jnp.dot(merged.astype(BF16), wo_ref[...], preferred_element_type=F32)
    o_ref[...] = _post_residual(x_ref[...], y, gate_ref[0], g_ref[...])


def _resident(shape):
    return pl.BlockSpec(shape, lambda i: (0,) * len(shape), pipeline_mode=pl.Buffered(1))


def _merge_out(ya, yb, proj, b_gate, wa, wb, wo, x, gate, g):
    tm = TM_MERGE
    per_b = SEQ // tm
    ga_blk, gb_blk = 4096 // D_MODEL, 6144 // D_MODEL
    return pl.pallas_call(
        _merge_out_kernel,
        grid=(TOKENS // tm,),
        in_specs=[pl.BlockSpec((tm, 1024), lambda i: (i, 0)),
                  pl.BlockSpec((tm, 1024), lambda i: (i, 0)),
                  pl.BlockSpec((tm, D_MODEL), lambda i: (i, ga_blk)),
                  pl.BlockSpec((tm, D_MODEL), lambda i: (i, gb_blk)),
                  _resident((2, D_MODEL)),
                  _resident((1024, D_MODEL)),
                  _resident((1024, D_MODEL)),
                  _resident((D_MODEL, D_MODEL)),
                  pl.BlockSpec((tm, D_MODEL), lambda i: (i, 0)),
                  pl.BlockSpec((1, 1, D_MODEL), lambda i: (i // per_b, 0, 0)),
                  _resident((1, D_MODEL))],
        out_specs=pl.BlockSpec((tm, D_MODEL), lambda i: (i, 0)),
        out_shape=jax.ShapeDtypeStruct((TOKENS, D_MODEL), F32),
        compiler_params=_cparams(("parallel",), V7X_VMEM_LIMIT),
        name="merge_out",
    )(ya, yb, proj, proj, b_gate, wa, wb, wo, x, gate, g.reshape(1, D_MODEL))


def _new_weights(te_ref, m):
    return jnp.logical_or(m == 0, te_ref[m] != te_ref[jnp.maximum(m - 1, 0)])


def _gu_kernel(te_ref, nu_ref, x_ref, wg_ref, wu_ref, o_ref, wg_s, wu_s):
    m = pl.program_id(1)

    @pl.when(m < nu_ref[0])
    def _():
        @pl.when(_new_weights(te_ref, m))
        def _():
            wg_s[...] = wg_ref[0].astype(BF16)
            wu_s[...] = wu_ref[0].astype(BF16)

        x = x_ref[...]
        g = jnp.dot(x, wg_s[...], preferred_element_type=F32)
        u = jnp.dot(x, wu_s[...], preferred_element_type=F32)
        o_ref[...] = (g * jax.nn.sigmoid(g) * u).astype(o_ref.dtype)

    @pl.when(m >= nu_ref[0])
    def _():
        o_ref[...] = jnp.zeros(o_ref.shape, o_ref.dtype)


def _grouped_gu(x, w_gu, te, nu, tm, tn):
    rows, k = x.shape
    d_ff = w_gu.shape[-1] // 2
    n_tiles = d_ff // tn
    clamp = lambda m, nu_ref: jnp.minimum(m, nu_ref[0] - 1)
    return pl.pallas_call(
        _gu_kernel,
        grid_spec=pltpu.PrefetchScalarGridSpec(
            num_scalar_prefetch=2,
            grid=(n_tiles, rows // tm),
            in_specs=[pl.BlockSpec((tm, k), lambda n, m, te_r, nu_r: (clamp(m, nu_r), 0)),
                      pl.BlockSpec((1, k, tn), lambda n, m, te_r, nu_r: (te_r[clamp(m, nu_r)], 0, n)),
                      pl.BlockSpec((1, k, tn), lambda n, m, te_r, nu_r: (te_r[clamp(m, nu_r)], 0, n + n_tiles))],
            out_specs=pl.BlockSpec((tm, tn), lambda n, m, te_r, nu_r: (m, n)),
            scratch_shapes=[pltpu.VMEM((k, tn), BF16), pltpu.VMEM((k, tn), BF16)]),
        out_shape=jax.ShapeDtypeStruct((rows, d_ff), BF16),
        compiler_params=_cparams(("arbitrary", "arbitrary"), V7X_VMEM_LIMIT),
        name="grouped_gate_up",
    )(te, nu, x, w_gu, w_gu)


def _down_kernel(te_ref, nu_ref, x_ref, w_ref, rs_ref, o_ref, w_s):
    m = pl.program_id(1)

    @pl.when(m < nu_ref[0])
    def _():
        @pl.when(_new_weights(te_ref, m))
        def _():
            w_s[...] = w_ref[0].astype(BF16)

        o_ref[...] = jnp.dot(x_ref[...], w_s[...], preferred_element_type=F32) * rs_ref[...]

    @pl.when(m >= nu_ref[0])
    def _():
        o_ref[...] = jnp.zeros(o_ref.shape, o_ref.dtype)


def _grouped_down(x, w_down, row_scale, te, nu, tm, tn):
    rows, k = x.shape
    n_out = w_down.shape[-1]
    clamp = lambda m, nu_ref: jnp.minimum(m, nu_ref[0] - 1)
    return pl.pallas_call(
        _down_kernel,
        grid_spec=pltpu.PrefetchScalarGridSpec(
            num_scalar_prefetch=2,
            grid=(n_out // tn, rows // tm),
            in_specs=[pl.BlockSpec((tm, k), lambda n, m, te_r, nu_r: (clamp(m, nu_r), 0)),
                      pl.BlockSpec((1, k, tn), lambda n, m, te_r, nu_r: (te_r[clamp(m, nu_r)], 0, n)),
                      pl.BlockSpec((tm, 1), lambda n, m, te_r, nu_r: (clamp(m, nu_r), 0))],
            out_specs=pl.BlockSpec((tm, tn), lambda n, m, te_r, nu_r: (m, n)),
            scratch_shapes=[pltpu.VMEM((k, tn), BF16)]),
        out_shape=jax.ShapeDtypeStruct((rows, n_out), F32),
        compiler_params=_cparams(("arbitrary", "arbitrary"), V7X_VMEM_LIMIT),
        name="grouped_down",
    )(te, nu, x, w_down, row_scale)


def _post_kernel(y_ref, x_ref, gate_ref, g_ref, o_ref):
    o_ref[...] = _post_residual(x_ref[...], y_ref[...], gate_ref[0], g_ref[...])


def _post(y, x, gate, g):
    tm = TM_NORM
    per_b = SEQ // tm
    return pl.pallas_call(
        _post_kernel,
        grid=(TOKENS // tm,),
        in_specs=[pl.BlockSpec((tm, D_MODEL), lambda i: (i, 0)),
                  pl.BlockSpec((tm, D_MODEL), lambda i: (i, 0)),
                  pl.BlockSpec((1, 1, D_MODEL), lambda i: (i // per_b, 0, 0)),
                  pl.BlockSpec((1, D_MODEL), lambda i: (0, 0))],
        out_specs=pl.BlockSpec((tm, D_MODEL), lambda i: (i, 0)),
        out_shape=jax.ShapeDtypeStruct((TOKENS, D_MODEL), F32),
        compiler_params=_cparams(("parallel",), V7X_VMEM_LIMIT),
        name="post_norm_residual",
    )(y, x, gate, g.reshape(1, D_MODEL))


def _row_copy(src_hbm, row, dst, slot, sem):
    return pltpu.make_async_copy(src_hbm.at[pl.ds(row, 1)], dst.at[pl.ds(slot, 1)], sem)


def _dispatch_kernel(src_ref, nu_ref, h_hbm, o_ref, buf, sem):
    i = pl.program_id(0)
    tm = buf.shape[0]

    @pl.when(i < nu_ref[0])
    def _():
        base = i * tm

        def issue(r, c):
            _row_copy(h_hbm, src_ref[base + r], buf, r, sem).start()
            return c

        def wait(r, c):
            _row_copy(h_hbm, 0, buf, r, sem).wait()
            return c

        lax.fori_loop(0, tm, issue, 0)
        lax.fori_loop(0, tm, wait, 0)
        o_ref[...] = buf[...].astype(o_ref.dtype)

    @pl.when(i >= nu_ref[0])
    def _():
        o_ref[...] = jnp.zeros(o_ref.shape, o_ref.dtype)


def _dispatch(h, src, nu):
    tm = TM_MOE
    return pl.pallas_call(
        _dispatch_kernel,
        grid_spec=pltpu.PrefetchScalarGridSpec(
            num_scalar_prefetch=2,
            grid=(N_SORT_PAD // tm,),
            in_specs=[pl.BlockSpec(memory_space=pl.ANY)],
            out_specs=pl.BlockSpec((tm, D_MODEL), lambda i, s_r, nu_r: (i, 0)),
            scratch_shapes=[pltpu.VMEM((tm, D_MODEL), F32), pltpu.SemaphoreType.DMA(())]),
        out_shape=jax.ShapeDtypeStruct((N_SORT_PAD, D_MODEL), BF16),
        compiler_params=_cparams(("arbitrary",)),
        name="moe_dispatch",
    )(src, nu, h)


def _combine_kernel(p0_ref, p1_ref, ys_hbm, x_ref, gate_ref, g_ref, o_ref, buf0, buf1, sem):
    i = pl.program_id(0)
    tm = buf0.shape[0]
    base = i * tm

    def issue(r, c):
        _row_copy(ys_hbm, p0_ref[base + r], buf0, r, sem).start()
        _row_copy(ys_hbm, p1_ref[base + r], buf1, r, sem).start()
        return c

    def wait(r, c):
        _row_copy(ys_hbm, 0, buf0, r, sem).wait()
        _row_copy(ys_hbm, 0, buf1, r, sem).wait()
        return c

    lax.fori_loop(0, tm, issue, 0)
    lax.fori_loop(0, tm, wait, 0)
    o_ref[...] = _post_residual(x_ref[...], buf0[...] + buf1[...], gate_ref[0], g_ref[...])


def _combine(ys, pos0, pos1, x, gate, g):
    tm = TM_COMBINE
    per_b = SEQ // tm
    return pl.pallas_call(
        _combine_kernel,
        grid_spec=pltpu.PrefetchScalarGridSpec(
            num_scalar_prefetch=2,
            grid=(TOKENS // tm,),
            in_specs=[pl.BlockSpec(memory_space=pl.ANY),
                      pl.BlockSpec((tm, D_MODEL), lambda i, a, b: (i, 0)),
                      pl.BlockSpec((1, 1, D_MODEL), lambda i, a, b: (i // per_b, 0, 0)),
                      pl.BlockSpec((1, D_MODEL), lambda i, a, b: (0, 0))],
            out_specs=pl.BlockSpec((tm, D_MODEL), lambda i, a, b: (i, 0)),
            scratch_shapes=[pltpu.VMEM((tm, D_MODEL), F32), pltpu.VMEM((tm, D_MODEL), F32),
                            pltpu.SemaphoreType.DMA(())]),
        out_shape=jax.ShapeDtypeStruct((TOKENS, D_MODEL), F32),
        compiler_params=_cparams(("arbitrary",)),
        name="moe_combine",
    )(pos0, pos1, ys, x, gate, g.reshape(1, D_MODEL))


def _routing_tables(top_i, top_w):
    tm = TM_MOE
    n_tiles = N_SORT_PAD // tm
    e_flat = top_i.reshape(-1)
    onehot = (e_flat[:, None] == jnp.arange(N_EXPERTS, dtype=jnp.int32)[None, :]).astype(jnp.int32)
    csum = jnp.cumsum(onehot, axis=0)
    rank = jnp.sum((csum - onehot) * onehot, axis=1)
    counts = csum[-1]
    padded = ((counts + tm - 1) // tm) * tm
    ends = jnp.cumsum(padded)
    offs = ends - padded
    pos = (offs[e_flat] + rank).astype(jnp.int32)
    tok = jnp.tile(jnp.arange(TOKENS, dtype=jnp.int32), TOP_K)
    src = jnp.zeros((N_SORT_PAD,), jnp.int32).at[pos].set(tok)
    row_scale = jnp.zeros((N_SORT_PAD,), F32).at[pos].set(top_w.reshape(-1)).reshape(N_SORT_PAD, 1)
    tile_start = jnp.arange(n_tiles, dtype=jnp.int32) * tm
    te = jnp.minimum(jnp.searchsorted(ends, tile_start, side="right"), N_EXPERTS - 1).astype(jnp.int32)
    nu = (ends[-1] // tm).astype(jnp.int32).reshape(1)
    return src, row_scale, te, nu, pos[:TOKENS], pos[TOKENS:]


def _pad_w_in(w):
    lat = MLA_Q_RANK + MLA_KV_RANK + MLA_ROPE
    pad = jnp.zeros((D_MODEL, 1024 - lat), w.dtype)
    return jnp.concatenate([w[:, :lat], pad, w[:, lat:]], axis=1).astype(BF16)


def _pad_w_q_up(w):
    w = w.reshape(MLA_Q_RANK, MLA_HEADS, MLA_NOPE + MLA_ROPE)
    pad = jnp.zeros((MLA_Q_RANK, MLA_HEADS, MLA_QK_PAD - MLA_NOPE - MLA_ROPE), w.dtype)
    return jnp.concatenate([w, pad], axis=2).reshape(MLA_Q_RANK, MLA_HEADS * MLA_QK_PAD).astype(BF16)


def _perm_w_kv_up(w):
    w = w.reshape(MLA_KV_RANK, MLA_HEADS, MLA_NOPE + MLA_V)
    k = w[:, :, :MLA_NOPE].reshape(MLA_KV_RANK, MLA_HEADS * MLA_NOPE)
    v = w[:, :, MLA_NOPE:].reshape(MLA_KV_RANK, MLA_HEADS * MLA_V)
    return jnp.concatenate([k, v], axis=1).astype(BF16)


def kernel(x, c, positions, ada_w, ada_b, norm_g, w_in, q_norm_g, kv_norm_g, w_q_up, w_kv_up, diff_lambda,
           diff_subln_g, w_branch_a, w_branch_b, b_gate, w_out, dense_w_gu, dense_w_down, router_w, moe_w_gu,
           moe_w_down):
    assert x.shape == (BATCH, SEQ, D_MODEL) and c.shape == (BATCH, D_MODEL)
    assert w_in.shape == (DEPTH, D_MODEL, 8000)
    xt = x.reshape(TOKENS, D_MODEL)
    cos, sin = _rope_tables(positions)
    mod = _adaln(c, ada_w, ada_b)
    dense_tiles = TOKENS // TM_DENSE
    te_dense = jnp.zeros((dense_tiles,), jnp.int32)
    nu_dense = jnp.full((1,), dense_tiles, jnp.int32)
    ones_rows = jnp.ones((TOKENS, 1), F32)

    for l in range(DEPTH):
        shift, scale, gate = (mod[l, 0, :, :, i * D_MODEL:(i + 1) * D_MODEL] for i in range(3))
        h = _norm_mod(xt, norm_g[l, 0], scale, shift)
        proj = _in_proj(h, _pad_w_in(w_in[l]), cos, sin)
        qa, ka, va = _mla_up(proj, q_norm_g[l], kv_norm_g[l], _pad_w_q_up(w_q_up[l]), _perm_w_kv_up(w_kv_up[l]),
                             cos, sin)
        ya = _mla_attn(qa, ka, va)
        yb = _diff_attn(proj, diff_lambda[l], diff_subln_g[l], l)
        xt = _merge_out(ya, yb, proj, b_gate[l], w_branch_a[l].astype(BF16), w_branch_b[l].astype(BF16),
                        w_out[l].astype(BF16), xt, gate, norm_g[l, 1])

        shift, scale, gate = (mod[l, 1, :, :, i * D_MODEL:(i + 1) * D_MODEL] for i in range(3))
        if l % 2 == 0:
            h = _norm_mod(xt, norm_g[l, 2], scale, shift)
            act = _grouped_gu(h, dense_w_gu[l // 2][None], te_dense, nu_dense, TM_DENSE, TN_DENSE_GU)
            y = _grouped_down(act, dense_w_down[l // 2][None], ones_rows, te_dense, nu_dense, TM_DENSE,
                              TN_DENSE_DOWN)
            xt = _post(y, xt, gate, norm_g[l, 3])
        else:
            h, top_i, top_w = _norm_mod_route(xt, norm_g[l, 2], scale, shift, router_w[l // 2])
            src, row_scale, te, nu, pos0, pos1 = _routing_tables(top_i, top_w)
            xs = _dispatch(h, src, nu)
            act = _grouped_gu(xs, moe_w_gu[l // 2], te, nu, TM_MOE, TN_MOE_GU)
            ys = _grouped_down(act, moe_w_down[l // 2], row_scale, te, nu, TM_MOE, TN_MOE_DOWN)
            xt = _combine(ys, pos0, pos1, xt, gate, norm_g[l, 3])
    return xt.reshape(BATCH, SEQ, D_MODEL)
```

```python
import functools
import math

import jax
import jax.numpy as jnp
import numpy as np
from jax import lax
from jax.experimental import pallas as pl
from jax.experimental.pallas import tpu as pltpu

F32 = jnp.float32
BF16 = jnp.bfloat16

D_MODEL = 2048
BATCH = 2
SEQ = 4096
TOKENS = BATCH * SEQ
DEPTH = 2
CHUNK = 64
ROPE_THETA = 10000.0
EPS = 1e-6
NEG_INF = -1e30

MLA_HEADS = 8
MLA_Q_RANK = 512
MLA_KV_RANK = 256
MLA_NOPE = 128
MLA_ROPE = 64
MLA_V = 128
MLA_QK_PAD = 256

DIFF_HEADS = 8
DIFF_HEAD_DIM = 64

D_FF_DENSE = 5632
N_EXPERTS = 8
TOP_K = 2
D_FF_EXPERT = 7168

D_IN_PAD = 8192

LANES = 128
V7X_VMEM_LIMIT = 56 * 1024 * 1024

TM_NORM = 512
TM_IN = 1024
TN_IN = 1024
TM_UP = 512
TQ = 512
TM_MERGE = 256
TM_DENSE = 512
TN_DENSE_GU = 512
TN_DENSE_DOWN = 512
TM_MOE = 256
TN_MOE_GU = 1024
TN_MOE_DOWN = 512
N_SORT_PAD = TOP_K * TOKENS + N_EXPERTS * TM_MOE
TM_COMBINE = 256


def _cparams(sem, vmem=None):
    return pltpu.CompilerParams(dimension_semantics=sem, vmem_limit_bytes=vmem)


def _nt_dot(a, b):
    return lax.dot_general(a, b, (((1,), (1,)), ((), ())), preferred_element_type=F32)


def _rms(y):
    return y * lax.rsqrt(jnp.mean(y * y, axis=-1, keepdims=True) + EPS)


def _rope_lanes(blk, cos, sin_signed, first_half):
    rot = jnp.where(first_half, pltpu.roll(blk, 96, 1), pltpu.roll(blk, 32, 1))
    return blk * cos + rot * sin_signed


def _first_half_mask(rows):
    lane = lax.broadcasted_iota(jnp.int32, (rows, LANES), 1)
    return (lane % 64) < 32


def _rope_tables_kernel(pos_ref, freq_ref, sign_ref, cos_ref, sin_ref):
    ang = pos_ref[...].astype(F32) * freq_ref[...]
    cos_ref[...] = jnp.cos(ang)
    sin_ref[...] = jnp.sin(ang) * sign_ref[...]


def _rope_tables(positions):
    half = MLA_ROPE // 2
    inv_freq = ROPE_THETA ** (-jnp.arange(0, MLA_ROPE, 2, dtype=F32) / MLA_ROPE)
    freq = jnp.tile(inv_freq, LANES // half).reshape(1, LANES)
    sign = jnp.tile(jnp.concatenate([-jnp.ones((half,), F32), jnp.ones((half,), F32)]), LANES // MLA_ROPE)
    sign = sign.reshape(1, LANES)
    pos = positions.reshape(TOKENS, 1)
    tm = 1024
    return pl.pallas_call(
        _rope_tables_kernel,
        grid=(TOKENS // tm,),
        in_specs=[pl.BlockSpec((tm, 1), lambda i: (i, 0)),
                  pl.BlockSpec((1, LANES), lambda i: (0, 0)),
                  pl.BlockSpec((1, LANES), lambda i: (0, 0))],
        out_specs=[pl.BlockSpec((tm, LANES), lambda i: (i, 0)),
                   pl.BlockSpec((tm, LANES), lambda i: (i, 0))],
        out_shape=[jax.ShapeDtypeStruct((TOKENS, LANES), F32)] * 2,
        compiler_params=_cparams(("parallel",)),
        name="rope_tables",
    )(pos, freq, sign)


def _adaln_kernel(c_ref, w_ref, b_ref, o_ref):
    c = c_ref[...]
    cs = (c * jax.nn.sigmoid(c)).astype(BF16)
    o_ref[0] = jnp.dot(cs, w_ref[0].astype(BF16), preferred_element_type=F32) + b_ref[0]


def _adaln(c, ada_w, ada_b):
    n_mod = DEPTH * 2
    w = ada_w.reshape(n_mod, D_MODEL, 3 * D_MODEL)
    b = ada_b.reshape(n_mod, 1, 3 * D_MODEL)
    c8 = jnp.zeros((8, D_MODEL), F32).at[:BATCH].set(c)
    tn = 1024
    out = pl.pallas_call(
        _adaln_kernel,
        grid=(n_mod, 3 * D_MODEL // tn),
        in_specs=[pl.BlockSpec((8, D_MODEL), lambda m, n: (0, 0)),
                  pl.BlockSpec((1, D_MODEL, tn), lambda m, n: (m, 0, n)),
                  pl.BlockSpec((1, 1, tn), lambda m, n: (m, 0, n))],
        out_specs=pl.BlockSpec((1, 8, tn), lambda m, n: (m, 0, n)),
        out_shape=jax.ShapeDtypeStruct((n_mod, 8, 3 * D_MODEL), F32),
        compiler_params=_cparams(("parallel", "parallel"), V7X_VMEM_LIMIT),
        name="adaln",
    )(c8, w, b)
    mod = out[:, :BATCH, :].reshape(DEPTH, 2, BATCH, 1, 3 * D_MODEL)
    return mod


def _norm_mod_kernel(x_ref, g_ref, sc_ref, sh_ref, h_ref):
    h = _rms(x_ref[...]) * g_ref[...] * (1.0 + sc_ref[0]) + sh_ref[0]
    h_ref[...] = h.astype(h_ref.dtype)


def _norm_mod_route_kernel(x_ref, g_ref, sc_ref, sh_ref, rwh_ref, rwl_ref, h_ref, idx_ref, wt_ref):
    h = _rms(x_ref[...]) * g_ref[...] * (1.0 + sc_ref[0]) + sh_ref[0]
    h_ref[...] = h
    h_hi = h.astype(BF16)
    h_lo = (h - h_hi.astype(F32)).astype(BF16)
    logits = (jnp.dot(h_hi, rwh_ref[...], preferred_element_type=F32)
              + jnp.dot(h_hi, rwl_ref[...], preferred_element_type=F32)
              + jnp.dot(h_lo, rwh_ref[...], preferred_element_type=F32))
    lt = logits.T[:N_EXPERTS, :]
    eid = lax.broadcasted_iota(jnp.int32, lt.shape, 0)
    m1 = jnp.max(lt, axis=0, keepdims=True)
    i1 = jnp.min(jnp.where(lt == m1, eid, N_EXPERTS), axis=0, keepdims=True)
    lt2 = jnp.where(eid == i1, -jnp.inf, lt)
    m2 = jnp.max(lt2, axis=0, keepdims=True)
    i2 = jnp.min(jnp.where(lt2 == m2, eid, N_EXPERTS), axis=0, keepdims=True)
    e2 = jnp.exp(m2 - m1)
    w1 = 1.0 / (1.0 + e2)
    idx_ref[0:1, :] = i1
    idx_ref[1:2, :] = i2
    wt_ref[0:1, :] = w1
    wt_ref[1:2, :] = e2 * w1


def _mod_specs(tm):
    per_b = SEQ // tm
    return [pl.BlockSpec((tm, D_MODEL), lambda i: (i, 0)),
            pl.BlockSpec((1, D_MODEL), lambda i: (0, 0)),
            pl.BlockSpec((1, 1, D_MODEL), lambda i: (i // per_b, 0, 0)),
            pl.BlockSpec((1, 1, D_MODEL), lambda i: (i // per_b, 0, 0))]


def _norm_mod(x, g, scale, shift):
    tm = TM_NORM
    return pl.pallas_call(
        _norm_mod_kernel,
        grid=(TOKENS // tm,),
        in_specs=_mod_specs(tm),
        out_specs=pl.BlockSpec((tm, D_MODEL), lambda i: (i, 0)),
        out_shape=jax.ShapeDtypeStruct((TOKENS, D_MODEL), BF16),
        compiler_params=_cparams(("parallel",)),
        name="norm_mod",
    )(x, g.reshape(1, D_MODEL), scale, shift)


def _norm_mod_route(x, g, scale, shift, router_w):
    tm = TM_NORM
    rw = jnp.zeros((D_MODEL, LANES), F32).at[:, :N_EXPERTS].set(router_w)
    rw_hi = rw.astype(BF16)
    rw_lo = (rw - rw_hi.astype(F32)).astype(BF16)
    return pl.pallas_call(
        _norm_mod_route_kernel,
        grid=(TOKENS // tm,),
        in_specs=_mod_specs(tm) + [pl.BlockSpec((D_MODEL, LANES), lambda i: (0, 0)),
                                   pl.BlockSpec((D_MODEL, LANES), lambda i: (0, 0))],
        out_specs=[pl.BlockSpec((tm, D_MODEL), lambda i: (i, 0)),
                   pl.BlockSpec((TOP_K, tm), lambda i: (0, i)),
                   pl.BlockSpec((TOP_K, tm), lambda i: (0, i))],
        out_shape=[jax.ShapeDtypeStruct((TOKENS, D_MODEL), F32),
                   jax.ShapeDtypeStruct((TOP_K, TOKENS), jnp.int32),
                   jax.ShapeDtypeStruct((TOP_K, TOKENS), F32)],
        compiler_params=_cparams(("parallel",)),
        name="norm_mod_route",
    )(x, g.reshape(1, D_MODEL), scale, shift, rw_hi, rw_lo)


def _in_proj_kernel(h_ref, w_ref, cos_ref, sin_ref, o_ref):
    n = pl.program_id(0)
    acc = jnp.dot(h_ref[...], w_ref[...], preferred_element_type=F32)
    tm, tn = acc.shape

    def rope_cols(lo, hi, mult):
        fh = _first_half_mask(tm)
        cos, sin = cos_ref[...], sin_ref[...]
        for c in range(lo // LANES, hi // LANES):
            blk = acc[:, c * LANES:(c + 1) * LANES]
            o_ref[:, c * LANES:(c + 1) * LANES] = (_rope_lanes(blk, cos, sin, fh) * mult).astype(o_ref.dtype)

    @pl.when(n == 0)
    def _():
        lat = MLA_Q_RANK + MLA_KV_RANK
        o_ref[:, :lat] = acc[:, :lat].astype(o_ref.dtype)
        rope_cols(lat, tn, 1.0)

    @pl.when(n == 1)
    def _():
        rope_cols(0, tn, DIFF_HEAD_DIM ** -0.5)

    @pl.when(n == 2)
    def _():
        rope_cols(0, tn, 1.0)

    @pl.when(n >= 3)
    def _():
        o_ref[...] = acc.astype(o_ref.dtype)


def _in_proj(h, w_in_p, cos, sin):
    tm, tn = TM_IN, TN_IN
    return pl.pallas_call(
        _in_proj_kernel,
        grid=(D_IN_PAD // tn, TOKENS // tm),
        in_specs=[pl.BlockSpec((tm, D_MODEL), lambda n, m: (m, 0)),
                  pl.BlockSpec((D_MODEL, tn), lambda n, m: (0, n)),
                  pl.BlockSpec((tm, LANES), lambda n, m: (m, 0)),
                  pl.BlockSpec((tm, LANES), lambda n, m: (m, 0))],
        out_specs=pl.BlockSpec((tm, tn), lambda n, m: (m, n)),
        out_shape=jax.ShapeDtypeStruct((TOKENS, D_IN_PAD), BF16),
        compiler_params=_cparams(("parallel", "parallel"), V7X_VMEM_LIMIT),
        name="in_proj",
    )(h, w_in_p, cos, sin)


def _mla_up_kernel(lat_ref, qg_ref, kvg_ref, wq_ref, wkv_ref, cos_ref, sin_ref, q_ref, k_ref, v_ref):
    tm = lat_ref.shape[0]
    scale = (MLA_NOPE + MLA_ROPE) ** -0.5
    q_lat = lat_ref[:, :MLA_Q_RANK].astype(F32)
    kv_lat = lat_ref[:, MLA_Q_RANK:MLA_Q_RANK + MLA_KV_RANK].astype(F32)
    k_pe = lat_ref[:, MLA_Q_RANK + MLA_KV_RANK:MLA_Q_RANK + MLA_KV_RANK + LANES]
    qn = (_rms(q_lat) * qg_ref[...]).astype(BF16)
    kvn = (_rms(kv_lat) * kvg_ref[...]).astype(BF16)
    q = jnp.dot(qn, wq_ref[...], preferred_element_type=F32)
    kv = jnp.dot(kvn, wkv_ref[...], preferred_element_type=F32)
    fh = _first_half_mask(tm)
    cos, sin = cos_ref[...], sin_ref[...]
    for h in range(MLA_HEADS):
        c0 = h * MLA_QK_PAD
        q_ref[:, c0:c0 + LANES] = (q[:, c0:c0 + LANES] * scale).astype(BF16)
        q_pe = _rope_lanes(q[:, c0 + LANES:c0 + 2 * LANES], cos, sin, fh)
        q_ref[:, c0 + LANES:c0 + 2 * LANES] = (q_pe * scale).astype(BF16)
        k_ref[:, c0:c0 + LANES] = kv[:, h * LANES:(h + 1) * LANES].astype(BF16)
        k_ref[:, c0 + LANES:c0 + 2 * LANES] = k_pe
    v_ref[...] = kv[:, MLA_HEADS * MLA_NOPE:].astype(BF16)


def _mla_up(proj, q_norm_g, kv_norm_g, wq_p, wkv_p, cos, sin):
    tm = TM_UP
    lat_w = 1024
    hq = MLA_HEADS * MLA_QK_PAD
    hv = MLA_HEADS * MLA_V
    const = lambda i: (0, 0)
    return pl.pallas_call(
        _mla_up_kernel,
        grid=(TOKENS // tm,),
        in_specs=[pl.BlockSpec((tm, lat_w), lambda i: (i, 0)),
                  pl.BlockSpec((1, MLA_Q_RANK), const),
                  pl.BlockSpec((1, MLA_KV_RANK), const),
                  pl.BlockSpec((MLA_Q_RANK, hq), const),
                  pl.BlockSpec((MLA_KV_RANK, 2 * hv), const),
                  pl.BlockSpec((tm, LANES), lambda i: (i, 0)),
                  pl.BlockSpec((tm, LANES), lambda i: (i, 0))],
        out_specs=[pl.BlockSpec((tm, hq), lambda i: (i, 0)),
                   pl.BlockSpec((tm, hq), lambda i: (i, 0)),
                   pl.BlockSpec((tm, hv), lambda i: (i, 0))],
        out_shape=[jax.ShapeDtypeStruct((TOKENS, hq), BF16),
                   jax.ShapeDtypeStruct((TOKENS, hq), BF16),
                   jax.ShapeDtypeStruct((TOKENS, hv), BF16)],
        compiler_params=_cparams(("parallel",), V7X_VMEM_LIMIT),
        name="mla_up",
    )(proj, q_norm_g.reshape(1, -1), kv_norm_g.reshape(1, -1), wq_p, wkv_p, cos, sin)


def _chunk_mask(tq):
    qc = lax.broadcasted_iota(jnp.int32, (tq, tq), 0) // CHUNK
    kc = lax.broadcasted_iota(jnp.int32, (tq, tq), 1) // CHUNK
    return kc <= qc


def _softmax_pv(s, vaug, m_s, acc_s):
    tq, tk = s.shape
    m_prev = m_s[...]
    m_new = jnp.maximum(m_prev, jnp.max(s, axis=1, keepdims=True))
    alpha = jnp.exp(m_prev - m_new)
    p = jnp.exp(s - jnp.tile(m_new, (1, tk // LANES)))
    pv = jnp.dot(p.astype(BF16), vaug, preferred_element_type=F32)
    acc_s[...] = jnp.tile(alpha, (1, 2)) * acc_s[...] + pv
    m_s[...] = m_new


def _attn_sweep(qi, scores, vaug_s, states):
    tk = TQ

    def consume(ss, j):
        vaug = vaug_s[pl.ds(pl.multiple_of(j * tk, tk), tk), :]
        for s, (m_s, acc_s) in zip(ss, states):
            _softmax_pv(s, vaug, m_s, acc_s)

    def body(j, ss):
        nxt = scores(j, False)
        consume(ss, jnp.where(j == 0, qi, j - 1))
        return nxt

    last = lax.fori_loop(0, qi, body, scores(qi, True))
    consume(last, jnp.where(qi == 0, qi, qi - 1))


def _init_attn_state(v_ref, vaug_s, states):
    @pl.when(pl.program_id(2) == 0)
    def _():
        vaug_s[:, :LANES] = v_ref[...]
        vaug_s[:, LANES:] = jnp.ones((vaug_s.shape[0], LANES), vaug_s.dtype)

    for m_s, acc_s in states:
        m_s[...] = jnp.full(m_s.shape, -jnp.inf, F32)
        acc_s[...] = jnp.zeros(acc_s.shape, F32)


def _attn_scratch(n_states):
    return ([pltpu.VMEM((SEQ, 2 * LANES), BF16)]
            + [pltpu.VMEM((TQ, LANES), F32), pltpu.VMEM((TQ, 2 * LANES), F32)] * n_states)


def _normalised(acc_s):
    return acc_s[:, :LANES] * (1.0 / acc_s[:, LANES:])


def _mla_attn_kernel(q_ref, k_ref, v_ref, o_ref, vaug_s, m_s, acc_s):
    tq = q_ref.shape[0]
    qi = pl.program_id(2)
    states = ((m_s, acc_s),)
    _init_attn_state(v_ref, vaug_s, states)
    q = q_ref[...]

    def scores(j, masked):
        s = _nt_dot(q, k_ref[pl.ds(pl.multiple_of(j * tq, tq), tq), :])
        if masked:
            s = jnp.where(_chunk_mask(tq), s, NEG_INF)
        return (s,)

    _attn_sweep(qi, scores, vaug_s, states)
    o_ref[...] = _normalised(acc_s).astype(o_ref.dtype)


def _mla_attn(qa, ka, va):
    tq = TQ
    nq = SEQ // tq
    return pl.pallas_call(
        _mla_attn_kernel,
        grid=(BATCH, MLA_HEADS, nq),
        in_specs=[pl.BlockSpec((tq, MLA_QK_PAD), lambda b, h, i: (b * nq + i, h)),
                  pl.BlockSpec((SEQ, MLA_QK_PAD), lambda b, h, i: (b, h)),
                  pl.BlockSpec((SEQ, MLA_V), lambda b, h, i: (b, h))],
        out_specs=pl.BlockSpec((tq, MLA_V), lambda b, h, i: (b * nq + i, h)),
        out_shape=jax.ShapeDtypeStruct((TOKENS, MLA_HEADS * MLA_V), BF16),
        scratch_shapes=_attn_scratch(1),
        compiler_params=_cparams(("parallel", "parallel", "arbitrary"), V7X_VMEM_LIMIT),
        name="mla_attn",
    )(qa, ka, va)


def _diff_attn_kernel(q_ref, k_ref, v_ref, lam_ref, g_ref, o_ref, vaug_s, m1, a1, m2, a2, *, lam_init):
    tq = q_ref.shape[0]
    qi = pl.program_id(2)
    states = ((m1, a1), (m2, a2))
    _init_attn_state(v_ref, vaug_s, states)
    q = q_ref[...]
    lane = lax.broadcasted_iota(jnp.int32, q.shape, 1)
    zero = jnp.zeros_like(q)
    q1 = jnp.where(lane < DIFF_HEAD_DIM, q, zero)
    q2 = jnp.where(lane >= DIFF_HEAD_DIM, q, zero)

    def scores(j, masked):
        k = k_ref[pl.ds(pl.multiple_of(j * tq, tq), tq), :]
        s1 = _nt_dot(q1, k)
        s2 = _nt_dot(q2, k)
        if masked:
            mask = _chunk_mask(tq)
            s1 = jnp.where(mask, s1, NEG_INF)
            s2 = jnp.where(mask, s2, NEG_INF)
        return (s1, s2)

    _attn_sweep(qi, scores, vaug_s, states)
    lp = lam_ref[...]
    lam = (jnp.exp(jnp.sum(lp[0:1] * lp[1:2], axis=1, keepdims=True))
           - jnp.exp(jnp.sum(lp[2:3] * lp[3:4], axis=1, keepdims=True)) + lam_init)
    od = _normalised(a1) - lam * _normalised(a2)
    o_ref[...] = (_rms(od) * g_ref[...] * (1.0 - lam_init)).astype(o_ref.dtype)


def _diff_attn(proj, diff_lambda, subln_g, layer):
    tq = TQ
    nq = SEQ // tq
    hd = 2 * DIFF_HEAD_DIM
    q0, k0, v0 = 1024 // hd, 2048 // hd, 3072 // hd
    lam_init = 0.8 - 0.6 * math.exp(-0.3 * layer)
    return pl.pallas_call(
        functools.partial(_diff_attn_kernel, lam_init=lam_init),
        grid=(BATCH, DIFF_HEADS, nq),
        in_specs=[pl.BlockSpec((tq, hd), lambda b, h, i: (b * nq + i, q0 + h)),
                  pl.BlockSpec((SEQ, hd), lambda b, h, i: (b, k0 + h)),
                  pl.BlockSpec((SEQ, hd), lambda b, h, i: (b, v0 + h)),
                  pl.BlockSpec((4, DIFF_HEAD_DIM), lambda b, h, i: (0, 0)),
                  pl.BlockSpec((1, hd), lambda b, h, i: (0, 0))],
        out_specs=pl.BlockSpec((tq, hd), lambda b, h, i: (b * nq + i, h)),
        out_shape=jax.ShapeDtypeStruct((TOKENS, DIFF_HEADS * hd), BF16),
        scratch_shapes=_attn_scratch(2),
        compiler_params=_cparams(("parallel", "parallel", "arbitrary"), V7X_VMEM_LIMIT),
        name="diff_attn",
    )(proj, proj, proj, diff_lambda, subln_g.reshape(1, hd))


def _post_residual(x, y, gate, g):
    return x + gate * (_rms(y) * g)


def _merge_out_kernel(ya_ref, yb_ref, ga_ref, gb_ref, bg_ref, wa_ref, wb_ref, wo_ref, x_ref, gate_ref, g_ref,
                      o_ref):
    a = jnp.dot(ya_ref[...], wa_ref[...], preferred_element_type=F32)
    b = jnp.dot(yb_ref[...], wb_ref[...], preferred_element_type=F32)
    bg = bg_ref[...]
    merged = (jax.nn.sigmoid(ga_ref[...].astype(F32) + bg[0:1]) * a
              + jax.nn.sigmoid(gb_ref[...].astype(F32) + bg[1:2]) * b)
    y = jnp.dot(merged.astype(BF16), wo_ref[...], preferred_element_type=F32)
    o_ref[...] = _post_residual(x_ref[...], y, gate_ref[0], g_ref[...])


def _resident(shape):
    return pl.BlockSpec(shape, lambda i: (0,) * len(shape), pipeline_mode=pl.Buffered(1))


def _merge_out(ya, yb, proj, b_gate, wa, wb, wo, x, gate, g):
    tm = TM_MERGE
    per_b = SEQ // tm
    ga_blk, gb_blk = 4096 // D_MODEL, 6144 // D_MODEL
    return pl.pallas_call(
        _merge_out_kernel,
        grid=(TOKENS // tm,),
        in_specs=[pl.BlockSpec((tm, 1024), lambda i: (i, 0)),
                  pl.BlockSpec((tm, 1024), lambda i: (i, 0)),
                  pl.BlockSpec((tm, D_MODEL), lambda i: (i, ga_blk)),
                  pl.BlockSpec((tm, D_MODEL), lambda i: (i, gb_blk)),
                  _resident((2, D_MODEL)),
                  _resident((1024, D_MODEL)),
                  _resident((1024, D_MODEL)),
                  _resident((D_MODEL, D_MODEL)),
                  pl.BlockSpec((tm, D_MODEL), lambda i: (i, 0)),
                  pl.BlockSpec((1, 1, D_MODEL), lambda i: (i // per_b, 0, 0)),
                  _resident((1, D_MODEL))],
        out_specs=pl.BlockSpec((tm, D_MODEL), lambda i: (i, 0)),
        out_shape=jax.ShapeDtypeStruct((TOKENS, D_MODEL), F32),
        compiler_params=_cparams(("parallel",), V7X_VMEM_LIMIT),
        name="merge_out",
    )(ya, yb, proj, proj, b_gate, wa, wb, wo, x, gate, g.reshape(1, D_MODEL))


def _new_weights(te_ref, m):
    return jnp.logical_or(m == 0, te_ref[m] != te_ref[jnp.maximum(m - 1, 0)])


def _gu_kernel(te_ref, nu_ref, x_ref, wg_ref, wu_ref, o_ref, wg_s, wu_s):
    m = pl.program_id(1)

    @pl.when(m < nu_ref[0])
    def _():
        @pl.when(_new_weights(te_ref, m))
        def _():
            wg_s[...] = wg_ref[0].astype(BF16)
            wu_s[...] = wu_ref[0].astype(BF16)

        x = x_ref[...]
        g = jnp.dot(x, wg_s[...], preferred_element_type=F32)
        u = jnp.dot(x, wu_s[...], preferred_element_type=F32)
        o_ref[...] = (g * jax.nn.sigmoid(g) * u).astype(o_ref.dtype)

    @pl.when(m >= nu_ref[0])
    def _():
        o_ref[...] = jnp.zeros(o_ref.shape, o_ref.dtype)


def _grouped_gu(x, w_gu, te, nu, tm, tn):
    rows, k = x.shape
    d_ff = w_gu.shape[-1] // 2
    n_tiles = d_ff // tn
    clamp = lambda m, nu_ref: jnp.minimum(m, nu_ref[0] - 1)
    return pl.pallas_call(
        _gu_kernel,
        grid_spec=pltpu.PrefetchScalarGridSpec(
            num_scalar_prefetch=2,
            grid=(n_tiles, rows // tm),
            in_specs=[pl.BlockSpec((tm, k), lambda n, m, te_r, nu_r: (clamp(m, nu_r), 0)),
                      pl.BlockSpec((1, k, tn), lambda n, m, te_r, nu_r: (te_r[clamp(m, nu_r)], 0, n)),
                      pl.BlockSpec((1, k, tn), lambda n, m, te_r, nu_r: (te_r[clamp(m, nu_r)], 0, n + n_tiles))],
            out_specs=pl.BlockSpec((tm, tn), lambda n, m, te_r, nu_r: (m, n)),
            scratch_shapes=[pltpu.VMEM((k, tn), BF16), pltpu.VMEM((k, tn), BF16)]),
        out_shape=jax.ShapeDtypeStruct((rows, d_ff), BF16),
        compiler_params=_cparams(("arbitrary", "arbitrary"), V7X_VMEM_LIMIT),
        name="grouped_gate_up",
    )(te, nu, x, w_gu, w_gu)


def _down_kernel(te_ref, nu_ref, x_ref, w_ref, rs_ref, o_ref, w_s):
    m = pl.program_id(1)

    @pl.when(m < nu_ref[0])
    def _():
        @pl.when(_new_weights(te_ref, m))
        def _():
            w_s[...] = w_ref[0].astype(BF16)

        o_ref[...] = jnp.dot(x_ref[...], w_s[...], preferred_element_type=F32) * rs_ref[...]

    @pl.when(m >= nu_ref[0])
    def _():
        o_ref[...] = jnp.zeros(o_ref.shape, o_ref.dtype)


def _grouped_down(x, w_down, row_scale, te, nu, tm, tn):
    rows, k = x.shape
    n_out = w_down.shape[-1]
    clamp = lambda m, nu_ref: jnp.minimum(m, nu_ref[0] - 1)
    return pl.pallas_call(
        _down_kernel,
        grid_spec=pltpu.PrefetchScalarGridSpec(
            num_scalar_prefetch=2,
            grid=(n_out // tn, rows // tm),
            in_specs=[pl.BlockSpec((tm, k), lambda n, m, te_r, nu_r: (clamp(m, nu_r), 0)),
                      pl.BlockSpec((1, k, tn), lambda n, m, te_r, nu_r: (te_r[clamp(m, nu_r)], 0, n)),
                      pl.BlockSpec((tm, 1), lambda n, m, te_r, nu_r: (clamp(m, nu_r), 0))],
            out_specs=pl.BlockSpec((tm, tn), lambda n, m, te_r, nu_r: (m, n)),
            scratch_shapes=[pltpu.VMEM((k, tn), BF16)]),
        out_shape=jax.ShapeDtypeStruct((rows, n_out), F32),
        compiler_params=_cparams(("arbitrary", "arbitrary"), V7X_VMEM_LIMIT),
        name="grouped_down",
    )(te, nu, x, w_down, row_scale)


def _post_kernel(y_ref, x_ref, gate_ref, g_ref, o_ref):
    o_ref[...] = _post_residual(x_ref[...], y_ref[...], gate_ref[0], g_ref[...])


def _post(y, x, gate, g):
    tm = TM_NORM
    per_b = SEQ // tm
    return pl.pallas_call(
        _post_kernel,
        grid=(TOKENS // tm,),
        in_specs=[pl.BlockSpec((tm, D_MODEL), lambda i: (i, 0)),
                  pl.BlockSpec((tm, D_MODEL), lambda i: (i, 0)),
                  pl.BlockSpec((1, 1, D_MODEL), lambda i: (i // per_b, 0, 0)),
                  pl.BlockSpec((1, D_MODEL), lambda i: (0, 0))],
        out_specs=pl.BlockSpec((tm, D_MODEL), lambda i: (i, 0)),
        out_shape=jax.ShapeDtypeStruct((TOKENS, D_MODEL), F32),
        compiler_params=_cparams(("parallel",), V7X_VMEM_LIMIT),
        name="post_norm_residual",
    )(y, x, gate, g.reshape(1, D_MODEL))


def _row_copy(src_hbm, row, dst, slot, sem):
    return pltpu.make_async_copy(src_hbm.at[pl.ds(row, 1)], dst.at[pl.ds(slot, 1)], sem)


def _dispatch_kernel(src_ref, nu_ref, h_hbm, o_ref, buf, sem):
    i = pl.program_id(0)
    tm = buf.shape[0]

    @pl.when(i < nu_ref[0])
    def _():
        base = i * tm

        def issue(r, c):
            _row_copy(h_hbm, src_ref[base + r], buf, r, sem).start()
            return c

        def wait(r, c):
            _row_copy(h_hbm, 0, buf, r, sem).wait()
            return c

        lax.fori_loop(0, tm, issue, 0)
        lax.fori_loop(0, tm, wait, 0)
        o_ref[...] = buf[...].astype(o_ref.dtype)

    @pl.when(i >= nu_ref[0])
    def _():
        o_ref[...] = jnp.zeros(o_ref.shape, o_ref.dtype)


def _dispatch(h, src, nu):
    tm = TM_MOE
    return pl.pallas_call(
        _dispatch_kernel,
        grid_spec=pltpu.PrefetchScalarGridSpec(
            num_scalar_prefetch=2,
            grid=(N_SORT_PAD // tm,),
            in_specs=[pl.BlockSpec(memory_space=pl.ANY)],
            out_specs=pl.BlockSpec((tm, D_MODEL), lambda i, s_r, nu_r: (i, 0)),
            scratch_shapes=[pltpu.VMEM((tm, D_MODEL), F32), pltpu.SemaphoreType.DMA(())]),
        out_shape=jax.ShapeDtypeStruct((N_SORT_PAD, D_MODEL), BF16),
        compiler_params=_cparams(("arbitrary",)),
        name="moe_dispatch",
    )(src, nu, h)


def _combine_kernel(p0_ref, p1_ref, ys_hbm, x_ref, gate_ref, g_ref, o_ref, buf0, buf1, sem):
    i = pl.program_id(0)
    tm = buf0.shape[0]
    base = i * tm

    def issue(r, c):
        _row_copy(ys_hbm, p0_ref[base + r], buf0, r, sem).start()
        _row_copy(ys_hbm, p1_ref[base + r], buf1, r, sem).start()
        return c

    def wait(r, c):
        _row_copy(ys_hbm, 0, buf0, r, sem).wait()
        _row_copy(ys_hbm, 0, buf1, r, sem).wait()
        return c

    lax.fori_loop(0, tm, issue, 0)
    lax.fori_loop(0, tm, wait, 0)
    o_ref[...] = _post_residual(x_ref[...], buf0[...] + buf1[...], gate_ref[0], g_ref[...])


def _combine(ys, pos0, pos1, x, gate, g):
    tm = TM_COMBINE
    per_b = SEQ // tm
    return pl.pallas_call(
        _combine_kernel,
        grid_spec=pltpu.PrefetchScalarGridSpec(
            num_scalar_prefetch=2,
            grid=(TOKENS // tm,),
            in_specs=[pl.BlockSpec(memory_space=pl.ANY),
                      pl.BlockSpec((tm, D_MODEL), lambda i, a, b: (i, 0)),
                      pl.BlockSpec((1, 1, D_MODEL), lambda i, a, b: (i // per_b, 0, 0)),
                      pl.BlockSpec((1, D_MODEL), lambda i, a, b: (0, 0))],
            out_specs=pl.BlockSpec((tm, D_MODEL), lambda i, a, b: (i, 0)),
            scratch_shapes=[pltpu.VMEM((tm, D_MODEL), F32), pltpu.VMEM((tm, D_MODEL), F32),
                            pltpu.SemaphoreType.DMA(())]),
        out_shape=jax.ShapeDtypeStruct((TOKENS, D_MODEL), F32),
        compiler_params=_cparams(("arbitrary",)),
        name="moe_combine",
    )(pos0, pos1, ys, x, gate, g.reshape(1, D_MODEL))


def _routing_tables(top_i, top_w):
    tm = TM_MOE
    n_tiles = N_SORT_PAD // tm
    e_flat = top_i.reshape(-1)
    onehot = (e_flat[:, None] == jnp.arange(N_EXPERTS, dtype=jnp.int32)[None, :]).astype(jnp.int32)
    csum = jnp.cumsum(onehot, axis=0)
    rank = jnp.sum((csum - onehot) * onehot, axis=1)
    counts = csum[-1]
    padded = ((counts + tm - 1) // tm) * tm
    ends = jnp.cumsum(padded)
    offs = ends - padded
    pos = (offs[e_flat] + rank).astype(jnp.int32)
    tok = jnp.tile(jnp.arange(TOKENS, dtype=jnp.int32), TOP_K)
    src = jnp.zeros((N_SORT_PAD,), jnp.int32).at[pos].set(tok)
    row_scale = jnp.zeros((N_SORT_PAD,), F32).at[pos].set(top_w.reshape(-1)).reshape(N_SORT_PAD, 1)
    tile_start = jnp.arange(n_tiles, dtype=jnp.int32) * tm
    te = jnp.sum((tile_start[:, None] >= ends[None, :]).astype(jnp.int32), axis=1)
    te = jnp.minimum(te, N_EXPERTS - 1)
    nu = (ends[-1] // tm).astype(jnp.int32).reshape(1)
    return src, row_scale, te, nu, pos[:TOKENS], pos[TOKENS:]


def _pad_w_in(w):
    lat = MLA_Q_RANK + MLA_KV_RANK + MLA_ROPE
    pad = jnp.zeros((D_MODEL, 1024 - lat), w.dtype)
    return jnp.concatenate([w[:, :lat], pad, w[:, lat:]], axis=1).astype(BF16)


def _pad_w_q_up(w):
    w = w.reshape(MLA_Q_RANK, MLA_HEADS, MLA_NOPE + MLA_ROPE)
    pad = jnp.zeros((MLA_Q_RANK, MLA_HEADS, MLA_QK_PAD - MLA_NOPE - MLA_ROPE), w.dtype)
    return jnp.concatenate([w, pad], axis=2).reshape(MLA_Q_RANK, MLA_HEADS * MLA_QK_PAD).astype(BF16)


def _perm_w_kv_up(w):
    w = w.reshape(MLA_KV_RANK, MLA_HEADS, MLA_NOPE + MLA_V)
    k = w[:, :, :MLA_NOPE].reshape(MLA_KV_RANK, MLA_HEADS * MLA_NOPE)
    v = w[:, :, MLA_NOPE:].reshape(MLA_KV_RANK, MLA_HEADS * MLA_V)
    return jnp.concatenate([k, v], axis=1).astype(BF16)


def kernel(x, c, positions, ada_w, ada_b, norm_g, w_in, q_norm_g, kv_norm_g, w_q_up, w_kv_up, diff_lambda,
           diff_subln_g, w_branch_a, w_branch_b, b_gate, w_out, dense_w_gu, dense_w_down, router_w, moe_w_gu,
           moe_w_down):
    assert x.shape == (BATCH, SEQ, D_MODEL) and c.shape == (BATCH, D_MODEL)
    assert w_in.shape == (DEPTH, D_MODEL, 8000)
    xt = x.reshape(TOKENS, D_MODEL)
    cos, sin = _rope_tables(positions)
    mod = _adaln(c, ada_w, ada_b)
    dense_tiles = TOKENS // TM_DENSE
    te_dense = jnp.zeros((dense_tiles,), jnp.int32)
    nu_dense = jnp.full((1,), dense_tiles, jnp.int32)
    ones_rows = jnp.ones((TOKENS, 1), F32)

    for l in range(DEPTH):
        shift, scale, gate = (mod[l, 0, :, :, i * D_MODEL:(i + 1) * D_MODEL] for i in range(3))
        h = _norm_mod(xt, norm_g[l, 0], scale, shift)
        proj = _in_proj(h, _pad_w_in(w_in[l]), cos, sin)
        qa, ka, va = _mla_up(proj, q_norm_g[l], kv_norm_g[l], _pad_w_q_up(w_q_up[l]), _perm_w_kv_up(w_kv_up[l]),
                             cos, sin)
        ya = _mla_attn(qa, ka, va)
        yb = _diff_attn(proj, diff_lambda[l], diff_subln_g[l], l)
        xt = _merge_out(ya, yb, proj, b_gate[l], w_branch_a[l].astype(BF16), w_branch_b[l].astype(BF16),
                        w_out[l].astype(BF16), xt, gate, norm_g[l, 1])

        shift, scale, gate = (mod[l, 1, :, :, i * D_MODEL:(i + 1) * D_MODEL] for i in range(3))
        if l % 2 == 0:
            h = _norm_mod(xt, norm_g[l, 2], scale, shift)
            act = _grouped_gu(h, dense_w_gu[l // 2][None], te_dense, nu_dense, TM_DENSE, TN_DENSE_GU)
            y = _grouped_down(act, dense_w_down[l // 2][None], ones_rows, te_dense, nu_dense, TM_DENSE,
                              TN_DENSE_DOWN)
            xt = _post(y, xt, gate, norm_g[l, 3])
        else:
            h, top_i, top_w = _norm_mod_route(xt, norm_g[l, 2], scale, shift, router_w[l // 2])
            src, row_scale, te, nu, pos0, pos1 = _routing_tables(top_i, top_w)
            xs = _dispatch(h, src, nu)
            act = _grouped_gu(xs, moe_w_gu[l // 2], te, nu, TM_MOE, TN_MOE_GU)
            ys = _grouped_down(act, moe_w_down[l // 2], row_scale, te, nu, TM_MOE, TN_MOE_DOWN)
            xt = _combine(ys, pos0, pos1, xt, gate, norm_g[l, 3])
    return xt.reshape(BATCH, SEQ, D_MODEL)
```

```python
import functools
import math

import jax
import jax.numpy as jnp
import numpy as np
from jax import lax
from jax.experimental import pallas as pl
from jax.experimental.pallas import tpu as pltpu

F32 = jnp.float32
BF16 = jnp.bfloat16

D_MODEL = 2048
BATCH = 2
SEQ = 4096
TOKENS = BATCH * SEQ
DEPTH = 2
CHUNK = 64
ROPE_THETA = 10000.0
EPS = 1e-6
NEG_INF = -1e30

MLA_HEADS = 8
MLA_Q_RANK = 512
MLA_KV_RANK = 256
MLA_NOPE = 128
MLA_ROPE = 64
MLA_V = 128
MLA_QK_PAD = 256

DIFF_HEADS = 8
DIFF_HEAD_DIM = 64

D_FF_DENSE = 5632
N_EXPERTS = 8
TOP_K = 2
D_FF_EXPERT = 7168

D_IN_PAD = 8192

LANES = 128
V7X_VMEM_LIMIT = 56 * 1024 * 1024

TM_NORM = 512
TM_IN = 1024
TN_IN = 1024
TM_UP = 512
TQ = 1024
TK = 512
TM_MERGE = 256
TM_DENSE_GU = 1024
TM_DENSE_DOWN = 512
TN_DENSE_GU = 512
TN_DENSE_DOWN = 512
TM_MOE = 512
TM_MOE_DOWN = 256
TM_DISPATCH = 256
TN_MOE_GU = 1024
TN_MOE_DOWN = 512
N_SORT_PAD = TOP_K * TOKENS + N_EXPERTS * TM_MOE
TM_COMBINE = 256


def _cparams(sem, vmem=None):
    return pltpu.CompilerParams(dimension_semantics=sem, vmem_limit_bytes=vmem)


def _nt_dot(a, b):
    return lax.dot_general(a, b, (((1,), (1,)), ((), ())), preferred_element_type=F32)


def _rms(y):
    return y * lax.rsqrt(jnp.mean(y * y, axis=-1, keepdims=True) + EPS)


def _rope_lanes(blk, cos, sin_signed, first_half):
    rot = jnp.where(first_half, pltpu.roll(blk, 96, 1), pltpu.roll(blk, 32, 1))
    return blk * cos + rot * sin_signed


def _first_half_mask(rows):
    lane = lax.broadcasted_iota(jnp.int32, (rows, LANES), 1)
    return (lane % 64) < 32


def _rope_tables_kernel(pos_ref, freq_ref, sign_ref, cos_ref, sin_ref):
    ang = pos_ref[...].astype(F32) * freq_ref[...]
    cos_ref[...] = jnp.cos(ang)
    sin_ref[...] = jnp.sin(ang) * sign_ref[...]


def _rope_tables(positions):
    half = MLA_ROPE // 2
    inv_freq = ROPE_THETA ** (-jnp.arange(0, MLA_ROPE, 2, dtype=F32) / MLA_ROPE)
    freq = jnp.tile(inv_freq, LANES // half).reshape(1, LANES)
    sign = jnp.tile(jnp.concatenate([-jnp.ones((half,), F32), jnp.ones((half,), F32)]), LANES // MLA_ROPE)
    sign = sign.reshape(1, LANES)
    pos = positions.reshape(TOKENS, 1)
    tm = 1024
    return pl.pallas_call(
        _rope_tables_kernel,
        grid=(TOKENS // tm,),
        in_specs=[pl.BlockSpec((tm, 1), lambda i: (i, 0)),
                  pl.BlockSpec((1, LANES), lambda i: (0, 0)),
                  pl.BlockSpec((1, LANES), lambda i: (0, 0))],
        out_specs=[pl.BlockSpec((tm, LANES), lambda i: (i, 0)),
                   pl.BlockSpec((tm, LANES), lambda i: (i, 0))],
        out_shape=[jax.ShapeDtypeStruct((TOKENS, LANES), F32)] * 2,
        compiler_params=_cparams(("parallel",)),
        name="rope_tables",
    )(pos, freq, sign)


def _adaln_kernel(c_ref, w_ref, b_ref, o_ref):
    c = c_ref[...]
    cs = (c * jax.nn.sigmoid(c)).astype(BF16)
    o_ref[0] = jnp.dot(cs, w_ref[0].astype(BF16), preferred_element_type=F32) + b_ref[0]


def _adaln(c, ada_w, ada_b):
    n_mod = DEPTH * 2
    w = ada_w.reshape(n_mod, D_MODEL, 3 * D_MODEL)
    b = ada_b.reshape(n_mod, 1, 3 * D_MODEL)
    c8 = jnp.zeros((8, D_MODEL), F32).at[:BATCH].set(c)
    tn = 1024
    out = pl.pallas_call(
        _adaln_kernel,
        grid=(n_mod, 3 * D_MODEL // tn),
        in_specs=[pl.BlockSpec((8, D_MODEL), lambda m, n: (0, 0)),
                  pl.BlockSpec((1, D_MODEL, tn), lambda m, n: (m, 0, n)),
                  pl.BlockSpec((1, 1, tn), lambda m, n: (m, 0, n))],
        out_specs=pl.BlockSpec((1, 8, tn), lambda m, n: (m, 0, n)),
        out_shape=jax.ShapeDtypeStruct((n_mod, 8, 3 * D_MODEL), F32),
        compiler_params=_cparams(("parallel", "parallel"), V7X_VMEM_LIMIT),
        name="adaln",
    )(c8, w, b)
    mod = out[:, :BATCH, :].reshape(DEPTH, 2, BATCH, 1, 3 * D_MODEL)
    return mod


def _norm_mod_kernel(x_ref, g_ref, sc_ref, sh_ref, h_ref):
    h = _rms(x_ref[...]) * g_ref[...] * (1.0 + sc_ref[0]) + sh_ref[0]
    h_ref[...] = h.astype(h_ref.dtype)


def _norm_mod_route_kernel(x_ref, g_ref, sc_ref, sh_ref, rwh_ref, rwl_ref, h_ref, idx_ref, wt_ref, rank_ref,
                           cnt_ref, cnt_s):
    @pl.when(pl.program_id(0) == 0)
    def _():
        cnt_s[...] = jnp.zeros(cnt_s.shape, F32)

    h = _rms(x_ref[...]) * g_ref[...] * (1.0 + sc_ref[0]) + sh_ref[0]
    h_ref[...] = h
    h_hi = h.astype(BF16)
    h_lo = (h - h_hi.astype(F32)).astype(BF16)
    logits = (jnp.dot(h_hi, rwh_ref[...], preferred_element_type=F32)
              + jnp.dot(h_hi, rwl_ref[...], preferred_element_type=F32)
              + jnp.dot(h_lo, rwh_ref[...], preferred_element_type=F32))
    lt = logits.T[:N_EXPERTS, :]
    eid = lax.broadcasted_iota(jnp.int32, lt.shape, 0)
    m1 = jnp.max(lt, axis=0, keepdims=True)
    i1 = jnp.min(jnp.where(lt == m1, eid, N_EXPERTS), axis=0, keepdims=True)
    lt2 = jnp.where(eid == i1, -jnp.inf, lt)
    m2 = jnp.max(lt2, axis=0, keepdims=True)
    i2 = jnp.min(jnp.where(lt2 == m2, eid, N_EXPERTS), axis=0, keepdims=True)
    e2 = jnp.exp(m2 - m1)
    w1 = 1.0 / (1.0 + e2)
    idx_ref[0:1, :] = i1
    idx_ref[1:2, :] = i2
    wt_ref[0:1, :] = w1
    wt_ref[1:2, :] = e2 * w1
    tm = lt.shape[1]
    chosen = jnp.logical_or(eid == i1, eid == i2)
    upper = (lax.broadcasted_iota(jnp.int32, (tm, tm), 0) < lax.broadcasted_iota(jnp.int32, (tm, tm), 1))
    before = jnp.dot(chosen.astype(BF16), upper.astype(BF16), preferred_element_type=F32) + cnt_s[:, 0:1]
    rank_ref[0:1, :] = jnp.sum(jnp.where(eid == i1, before, 0.0), axis=0, keepdims=True).astype(jnp.int32)
    rank_ref[1:2, :] = jnp.sum(jnp.where(eid == i2, before, 0.0), axis=0, keepdims=True).astype(jnp.int32)
    cnt_s[...] = cnt_s[...] + jnp.sum(chosen.astype(F32), axis=1, keepdims=True)
    cnt_ref[...] = cnt_s[...]


def _mod_specs(tm):
    per_b = SEQ // tm
    return [pl.BlockSpec((tm, D_MODEL), lambda i: (i, 0)),
            pl.BlockSpec((1, D_MODEL), lambda i: (0, 0)),
            pl.BlockSpec((1, 1, D_MODEL), lambda i: (i // per_b, 0, 0)),
            pl.BlockSpec((1, 1, D_MODEL), lambda i: (i // per_b, 0, 0))]


def _norm_mod(x, g, scale, shift):
    tm = TM_NORM
    return pl.pallas_call(
        _norm_mod_kernel,
        grid=(TOKENS // tm,),
        in_specs=_mod_specs(tm),
        out_specs=pl.BlockSpec((tm, D_MODEL), lambda i: (i, 0)),
        out_shape=jax.ShapeDtypeStruct((TOKENS, D_MODEL), BF16),
        compiler_params=_cparams(("parallel",)),
        name="norm_mod",
    )(x, g.reshape(1, D_MODEL), scale, shift)


def _norm_mod_route(x, g, scale, shift, router_w):
    tm = TM_NORM
    rw = jnp.zeros((D_MODEL, LANES), F32).at[:, :N_EXPERTS].set(router_w)
    rw_hi = rw.astype(BF16)
    rw_lo = (rw - rw_hi.astype(F32)).astype(BF16)
    return pl.pallas_call(
        _norm_mod_route_kernel,
        grid=(TOKENS // tm,),
        in_specs=_mod_specs(tm) + [pl.BlockSpec((D_MODEL, LANES), lambda i: (0, 0)),
                                   pl.BlockSpec((D_MODEL, LANES), lambda i: (0, 0))],
        out_specs=[pl.BlockSpec((tm, D_MODEL), lambda i: (i, 0)),
                   pl.BlockSpec((TOP_K, tm), lambda i: (0, i)),
                   pl.BlockSpec((TOP_K, tm), lambda i: (0, i)),
                   pl.BlockSpec((TOP_K, tm), lambda i: (0, i)),
                   pl.BlockSpec((N_EXPERTS, LANES), lambda i: (0, 0))],
        out_shape=[jax.ShapeDtypeStruct((TOKENS, D_MODEL), F32),
                   jax.ShapeDtypeStruct((TOP_K, TOKENS), jnp.int32),
                   jax.ShapeDtypeStruct((TOP_K, TOKENS), F32),
                   jax.ShapeDtypeStruct((TOP_K, TOKENS), jnp.int32),
                   jax.ShapeDtypeStruct((N_EXPERTS, LANES), F32)],
        scratch_shapes=[pltpu.VMEM((N_EXPERTS, LANES), F32)],
        compiler_params=_cparams(("arbitrary",)),
        name="norm_mod_route",
    )(x, g.reshape(1, D_MODEL), scale, shift, rw_hi, rw_lo)


def _in_proj_kernel(h_ref, w_ref, cos_ref, sin_ref, o_ref):
    n = pl.program_id(0)
    acc = jnp.dot(h_ref[...], w_ref[...], preferred_element_type=F32)
    tm, tn = acc.shape

    def rope_cols(lo, hi, mult):
        fh = _first_half_mask(tm)
        cos, sin = cos_ref[...], sin_ref[...]
        for c in range(lo // LANES, hi // LANES):
            blk = acc[:, c * LANES:(c + 1) * LANES]
            o_ref[:, c * LANES:(c + 1) * LANES] = (_rope_lanes(blk, cos, sin, fh) * mult).astype(o_ref.dtype)

    @pl.when(n == 0)
    def _():
        lat = MLA_Q_RANK + MLA_KV_RANK
        o_ref[:, :lat] = acc[:, :lat].astype(o_ref.dtype)
        rope_cols(lat, tn, 1.0)

    @pl.when(n == 1)
    def _():
        rope_cols(0, tn, DIFF_HEAD_DIM ** -0.5)

    @pl.when(n == 2)
    def _():
        rope_cols(0, tn, 1.0)

    @pl.when(n >= 3)
    def _():
        o_ref[...] = acc.astype(o_ref.dtype)


def _in_proj(h, w_in_p, cos, sin):
    tm, tn = TM_IN, TN_IN
    return pl.pallas_call(
        _in_proj_kernel,
        grid=(D_IN_PAD // tn, TOKENS // tm),
        in_specs=[pl.BlockSpec((tm, D_MODEL), lambda n, m: (m, 0)),
                  pl.BlockSpec((D_MODEL, tn), lambda n, m: (0, n)),
                  pl.BlockSpec((tm, LANES), lambda n, m: (m, 0)),
                  pl.BlockSpec((tm, LANES), lambda n, m: (m, 0))],
        out_specs=pl.BlockSpec((tm, tn), lambda n, m: (m, n)),
        out_shape=jax.ShapeDtypeStruct((TOKENS, D_IN_PAD), BF16),
        compiler_params=_cparams(("parallel", "parallel"), V7X_VMEM_LIMIT),
        name="in_proj",
    )(h, w_in_p, cos, sin)


def _mla_up_kernel(lat_ref, qg_ref, kvg_ref, wq_ref, wkv_ref, cos_ref, sin_ref, q_ref, k_ref, v_ref):
    tm = lat_ref.shape[0]
    scale = (MLA_NOPE + MLA_ROPE) ** -0.5
    q_lat = lat_ref[:, :MLA_Q_RANK].astype(F32)
    kv_lat = lat_ref[:, MLA_Q_RANK:MLA_Q_RANK + MLA_KV_RANK].astype(F32)
    k_pe = lat_ref[:, MLA_Q_RANK + MLA_KV_RANK:MLA_Q_RANK + MLA_KV_RANK + LANES]
    qn = (_rms(q_lat) * qg_ref[...]).astype(BF16)
    kvn = (_rms(kv_lat) * kvg_ref[...]).astype(BF16)
    q = jnp.dot(qn, wq_ref[...], preferred_element_type=F32)
    kv = jnp.dot(kvn, wkv_ref[...], preferred_element_type=F32)
    fh = _first_half_mask(tm)
    cos, sin = cos_ref[...], sin_ref[...]
    for h in range(MLA_HEADS):
        c0 = h * MLA_QK_PAD
        q_ref[:, c0:c0 + LANES] = (q[:, c0:c0 + LANES] * scale).astype(BF16)
        q_pe = _rope_lanes(q[:, c0 + LANES:c0 + 2 * LANES], cos, sin, fh)
        q_ref[:, c0 + LANES:c0 + 2 * LANES] = (q_pe * scale).astype(BF16)
        k_ref[:, c0:c0 + LANES] = kv[:, h * LANES:(h + 1) * LANES].astype(BF16)
        k_ref[:, c0 + LANES:c0 + 2 * LANES] = k_pe
    v_ref[...] = kv[:, MLA_HEADS * MLA_NOPE:].astype(BF16)


def _mla_up(proj, q_norm_g, kv_norm_g, wq_p, wkv_p, cos, sin):
    tm = TM_UP
    lat_w = 1024
    hq = MLA_HEADS * MLA_QK_PAD
    hv = MLA_HEADS * MLA_V
    const = lambda i: (0, 0)
    return pl.pallas_call(
        _mla_up_kernel,
        grid=(TOKENS // tm,),
        in_specs=[pl.BlockSpec((tm, lat_w), lambda i: (i, 0)),
                  pl.BlockSpec((1, MLA_Q_RANK), const),
                  pl.BlockSpec((1, MLA_KV_RANK), const),
                  pl.BlockSpec((MLA_Q_RANK, hq), const),
                  pl.BlockSpec((MLA_KV_RANK, 2 * hv), const),
                  pl.BlockSpec((tm, LANES), lambda i: (i, 0)),
                  pl.BlockSpec((tm, LANES), lambda i: (i, 0))],
        out_specs=[pl.BlockSpec((tm, hq), lambda i: (i, 0)),
                   pl.BlockSpec((tm, hq), lambda i: (i, 0)),
                   pl.BlockSpec((tm, hv), lambda i: (i, 0))],
        out_shape=[jax.ShapeDtypeStruct((TOKENS, hq), BF16),
                   jax.ShapeDtypeStruct((TOKENS, hq), BF16),
                   jax.ShapeDtypeStruct((TOKENS, hv), BF16)],
        compiler_params=_cparams(("parallel",), V7X_VMEM_LIMIT),
        name="mla_up",
    )(proj, q_norm_g.reshape(1, -1), kv_norm_g.reshape(1, -1), wq_p, wkv_p, cos, sin)


def _masked_scores(s, diag):
    tq, tk = s.shape
    qc = lax.broadcasted_iota(jnp.int32, (tq, tk), 0) // CHUNK
    kc = (lax.broadcasted_iota(jnp.int32, (tq, tk), 1) + diag * tk) // CHUNK
    return jnp.where(kc <= qc, s, NEG_INF)


def _key_rows(j):
    return pl.ds(pl.multiple_of(j * TK, TK), TK)


def _softmax_pv(s, vaug, m_s, acc_s):
    tq, tk = s.shape
    m_prev = m_s[...]
    m_new = jnp.maximum(m_prev, jnp.max(s, axis=1, keepdims=True))
    alpha = jnp.exp(m_prev - m_new)
    p = jnp.exp(s - jnp.tile(m_new, (1, tk // LANES)))
    pv = jnp.dot(p.astype(BF16), vaug, preferred_element_type=F32)
    acc_s[...] = jnp.tile(alpha, (1, 2)) * acc_s[...] + pv
    m_s[...] = m_new


def _attn_sweep(qi, scores, vaug_s, states):
    n_diag = TQ // TK
    n_full = qi * n_diag

    def consume(ss, j):
        vaug = vaug_s[_key_rows(j), :]
        for s, (m_s, acc_s) in zip(ss, states):
            _softmax_pv(s, vaug, m_s, acc_s)

    pending, idx = scores(n_full, 0), n_full
    for d in range(1, n_diag):
        nxt = scores(n_full + d, d)
        consume(pending, idx)
        pending, idx = nxt, n_full + d

    def body(j, carry):
        nxt = scores(j, None)
        consume(*carry)
        return nxt, j

    consume(*lax.fori_loop(0, n_full, body, (pending, idx)))


def _init_attn_state(v_ref, vaug_s, states):
    @pl.when(pl.program_id(2) == 0)
    def _():
        vaug_s[:, :LANES] = v_ref[...]
        vaug_s[:, LANES:] = jnp.ones((vaug_s.shape[0], LANES), vaug_s.dtype)

    for m_s, acc_s in states:
        m_s[...] = jnp.full(m_s.shape, -jnp.inf, F32)
        acc_s[...] = jnp.zeros(acc_s.shape, F32)


def _attn_scratch(n_states):
    return ([pltpu.VMEM((SEQ, 2 * LANES), BF16)]
            + [pltpu.VMEM((TQ, LANES), F32), pltpu.VMEM((TQ, 2 * LANES), F32)] * n_states)


def _normalised(acc_s):
    return acc_s[:, :LANES] * (1.0 / acc_s[:, LANES:])


def _mla_attn_kernel(q_ref, k_ref, v_ref, o_ref, vaug_s, m_s, acc_s):
    tq = q_ref.shape[0]
    qi = pl.program_id(2)
    states = ((m_s, acc_s),)
    _init_attn_state(v_ref, vaug_s, states)
    q = q_ref[...]

    def scores(j, diag):
        s = _nt_dot(q, k_ref[_key_rows(j), :])
        return (s if diag is None else _masked_scores(s, diag),)

    _attn_sweep(qi, scores, vaug_s, states)
    o_ref[...] = _normalised(acc_s).astype(o_ref.dtype)


def _mla_attn(qa, ka, va):
    tq = TQ
    nq = SEQ // tq
    return pl.pallas_call(
        _mla_attn_kernel,
        grid=(BATCH, MLA_HEADS, nq),
        in_specs=[pl.BlockSpec((tq, MLA_QK_PAD), lambda b, h, i: (b * nq + i, h)),
                  pl.BlockSpec((SEQ, MLA_QK_PAD), lambda b, h, i: (b, h)),
                  pl.BlockSpec((SEQ, MLA_V), lambda b, h, i: (b, h))],
        out_specs=pl.BlockSpec((tq, MLA_V), lambda b, h, i: (b * nq + i, h)),
        out_shape=jax.ShapeDtypeStruct((TOKENS, MLA_HEADS * MLA_V), BF16),
        scratch_shapes=_attn_scratch(1),
        compiler_params=_cparams(("parallel", "parallel", "arbitrary"), V7X_VMEM_LIMIT),
        name="mla_attn",
    )(qa, ka, va)


def _diff_attn_kernel(q_ref, k_ref, v_ref, lam_ref, g_ref, o_ref, vaug_s, m1, a1, m2, a2, *, lam_init):
    tq = q_ref.shape[0]
    qi = pl.program_id(2)
    states = ((m1, a1), (m2, a2))
    _init_attn_state(v_ref, vaug_s, states)
    q = q_ref[...]
    lane = lax.broadcasted_iota(jnp.int32, q.shape, 1)
    zero = jnp.zeros_like(q)
    q1 = jnp.where(lane < DIFF_HEAD_DIM, q, zero)
    q2 = jnp.where(lane >= DIFF_HEAD_DIM, q, zero)

    def scores(j, diag):
        k = k_ref[_key_rows(j), :]
        s1 = _nt_dot(q1, k)
        s2 = _nt_dot(q2, k)
        if diag is not None:
            s1, s2 = _masked_scores(s1, diag), _masked_scores(s2, diag)
        return (s1, s2)

    _attn_sweep(qi, scores, vaug_s, states)
    lp = lam_ref[...]
    lam = (jnp.exp(jnp.sum(lp[0:1] * lp[1:2], axis=1, keepdims=True))
           - jnp.exp(jnp.sum(lp[2:3] * lp[3:4], axis=1, keepdims=True)) + lam_init)
    od = _normalised(a1) - lam * _normalised(a2)
    o_ref[...] = (_rms(od) * g_ref[...] * (1.0 - lam_init)).astype(o_ref.dtype)


def _diff_attn(proj, diff_lambda, subln_g, layer):
    tq = TQ
    nq = SEQ // tq
    hd = 2 * DIFF_HEAD_DIM
    q0, k0, v0 = 1024 // hd, 2048 // hd, 3072 // hd
    lam_init = 0.8 - 0.6 * math.exp(-0.3 * layer)
    return pl.pallas_call(
        functools.partial(_diff_attn_kernel, lam_init=lam_init),
        grid=(BATCH, DIFF_HEADS, nq),
        in_specs=[pl.BlockSpec((tq, hd), lambda b, h, i: (b * nq + i, q0 + h)),
                  pl.BlockSpec((SEQ, hd), lambda b, h, i: (b, k0 + h)),
                  pl.BlockSpec((SEQ, hd), lambda b, h, i: (b, v0 + h)),
                  pl.BlockSpec((4, DIFF_HEAD_DIM), lambda b, h, i: (0, 0)),
                  pl.BlockSpec((1, hd), lambda b, h, i: (0, 0))],
        out_specs=pl.BlockSpec((tq, hd), lambda b, h, i: (b * nq + i, h)),
        out_shape=jax.ShapeDtypeStruct((TOKENS, DIFF_HEADS * hd), BF16),
        scratch_shapes=_attn_scratch(2),
        compiler_params=_cparams(("parallel", "parallel", "arbitrary"), V7X_VMEM_LIMIT),
        name="diff_attn",
    )(proj, proj, proj, diff_lambda, subln_g.reshape(1, hd))


def _post_residual(x, y, gate, g):
    return x + gate * (_rms(y) * g)


def _merge_out_kernel(ya_ref, yb_ref, ga_ref, gb_ref, bg_ref, wa_ref, wb_ref, wo_ref, x_ref, gate_ref, g_ref,
                      o_ref):
    a = jnp.dot(ya_ref[...], wa_ref[...], preferred_element_type=F32)
    b = jnp.dot(yb_ref[...], wb_ref[...], preferred_element_type=F32)
    bg = bg_ref[...]
    merged = (jax.nn.sigmoid(ga_ref[...].astype(F32) + bg[0:1]) * a
              + jax.nn.sigmoid(gb_ref[...].astype(F32) + bg[1:2]) * b)
    y = jnp.dot(merged.astype(BF16), wo_ref[...], preferred_element_type=F32)
    o_ref[...] = _post_residual(x_ref[...], y, gate_ref[0], g_ref[...])


def _resident(shape):
    return pl.BlockSpec(shape, lambda i: (0,) * len(shape), pipeline_mode=pl.Buffered(1))


def _merge_out(ya, yb, proj, b_gate, wa, wb, wo, x, gate, g):
    tm = TM_MERGE
    per_b = SEQ // tm
    ga_blk, gb_blk = 4096 // D_MODEL, 6144 // D_MODEL
    return pl.pallas_call(
        _merge_out_kernel,
        grid=(TOKENS // tm,),
        in_specs=[pl.BlockSpec((tm, 1024), lambda i: (i, 0)),
                  pl.BlockSpec((tm, 1024), lambda i: (i, 0)),
                  pl.BlockSpec((tm, D_MODEL), lambda i: (i, ga_blk)),
                  pl.BlockSpec((tm, D_MODEL), lambda i: (i, gb_blk)),
                  _resident((2, D_MODEL)),
                  _resident((1024, D_MODEL)),
                  _resident((1024, D_MODEL)),
                  _resident((D_MODEL, D_MODEL)),
                  pl.BlockSpec((tm, D_MODEL), lambda i: (i, 0)),
                  pl.BlockSpec((1, 1, D_MODEL), lambda i: (i // per_b, 0, 0)),
                  _resident((1, D_MODEL))],
        out_specs=pl.BlockSpec((tm, D_MODEL), lambda i: (i, 0)),
        out_shape=jax.ShapeDtypeStruct((TOKENS, D_MODEL), F32),
        compiler_params=_cparams(("parallel",), V7X_VMEM_LIMIT),
        name="merge_out",
    )(ya, yb, proj, proj, b_gate, wa, wb, wo, x, gate, g.reshape(1, D_MODEL))


def _new_weights(te_ref, m):
    return jnp.logical_or(m == 0, te_ref[m] != te_ref[jnp.maximum(m - 1, 0)])


def _gu_kernel(te_ref, nu_ref, x_ref, wg_ref, wu_ref, o_ref, wg_s, wu_s):
    m = pl.program_id(1)

    @pl.when(m < nu_ref[0])
    def _():
        @pl.when(_new_weights(te_ref, m))
        def _():
            wg_s[...] = wg_ref[0].astype(BF16)
            wu_s[...] = wu_ref[0].astype(BF16)

        x = x_ref[...]
        g = jnp.dot(x, wg_s[...], preferred_element_type=F32)
        u = jnp.dot(x, wu_s[...], preferred_element_type=F32)
        o_ref[...] = (g * jax.nn.sigmoid(g) * u).astype(o_ref.dtype)

    @pl.when(m >= nu_ref[0])
    def _():
        o_ref[...] = jnp.zeros(o_ref.shape, o_ref.dtype)


def _grouped_gu(x, w_gu, te, nu, tm, tn):
    rows, k = x.shape
    d_ff = w_gu.shape[-1] // 2
    n_tiles = d_ff // tn
    clamp = lambda m, nu_ref: jnp.minimum(m, nu_ref[0] - 1)
    return pl.pallas_call(
        _gu_kernel,
        grid_spec=pltpu.PrefetchScalarGridSpec(
            num_scalar_prefetch=2,
            grid=(n_tiles, rows // tm),
            in_specs=[pl.BlockSpec((tm, k), lambda n, m, te_r, nu_r: (clamp(m, nu_r), 0)),
                      pl.BlockSpec((1, k, tn), lambda n, m, te_r, nu_r: (te_r[clamp(m, nu_r)], 0, n)),
                      pl.BlockSpec((1, k, tn), lambda n, m, te_r, nu_r: (te_r[clamp(m, nu_r)], 0, n + n_tiles))],
            out_specs=pl.BlockSpec((tm, tn), lambda n, m, te_r, nu_r: (m, n)),
            scratch_shapes=[pltpu.VMEM((k, tn), BF16), pltpu.VMEM((k, tn), BF16)]),
        out_shape=jax.ShapeDtypeStruct((rows, d_ff), BF16),
        compiler_params=_cparams(("arbitrary", "arbitrary"), V7X_VMEM_LIMIT),
        name="grouped_gate_up",
    )(te, nu, x, w_gu, w_gu)


def _down_kernel(te_ref, nu_ref, x_ref, w_ref, o_ref, w_s):
    m = pl.program_id(1)

    @pl.when(m < nu_ref[0])
    def _():
        @pl.when(_new_weights(te_ref, m))
        def _():
            w_s[...] = w_ref[0].astype(BF16)

        o_ref[...] = jnp.dot(x_ref[...], w_s[...], preferred_element_type=F32)

    @pl.when(m >= nu_ref[0])
    def _():
        o_ref[...] = jnp.zeros(o_ref.shape, o_ref.dtype)


def _grouped_down(x, w_down, te, nu, tm, tn):
    rows, k = x.shape
    n_out = w_down.shape[-1]
    clamp = lambda m, nu_ref: jnp.minimum(m, nu_ref[0] - 1)
    return pl.pallas_call(
        _down_kernel,
        grid_spec=pltpu.PrefetchScalarGridSpec(
            num_scalar_prefetch=2,
            grid=(n_out // tn, rows // tm),
            in_specs=[pl.BlockSpec((tm, k), lambda n, m, te_r, nu_r: (clamp(m, nu_r), 0)),
                      pl.BlockSpec((1, k, tn), lambda n, m, te_r, nu_r: (te_r[clamp(m, nu_r)], 0, n))],
            out_specs=pl.BlockSpec((tm, tn), lambda n, m, te_r, nu_r: (m, n)),
            scratch_shapes=[pltpu.VMEM((k, tn), BF16)]),
        out_shape=jax.ShapeDtypeStruct((rows, n_out), F32),
        compiler_params=_cparams(("arbitrary", "arbitrary"), V7X_VMEM_LIMIT),
        name="grouped_down",
    )(te, nu, x, w_down)


def _post_kernel(y_ref, x_ref, gate_ref, g_ref, o_ref):
    o_ref[...] = _post_residual(x_ref[...], y_ref[...], gate_ref[0], g_ref[...])


def _post(y, x, gate, g):
    tm = TM_NORM
    per_b = SEQ // tm
    return pl.pallas_call(
        _post_kernel,
        grid=(TOKENS // tm,),
        in_specs=[pl.BlockSpec((tm, D_MODEL), lambda i: (i, 0)),
                  pl.BlockSpec((tm, D_MODEL), lambda i: (i, 0)),
                  pl.BlockSpec((1, 1, D_MODEL), lambda i: (i // per_b, 0, 0)),
                  pl.BlockSpec((1, D_MODEL), lambda i: (0, 0))],
        out_specs=pl.BlockSpec((tm, D_MODEL), lambda i: (i, 0)),
        out_shape=jax.ShapeDtypeStruct((TOKENS, D_MODEL), F32),
        compiler_params=_cparams(("parallel",), V7X_VMEM_LIMIT),
        name="post_norm_residual",
    )(y, x, gate, g.reshape(1, D_MODEL))


def _row_copy(src_hbm, row, dst, slot, sem):
    return pltpu.make_async_copy(src_hbm.at[pl.ds(row, 1)], dst.at[pl.ds(slot, 1)], sem)


def _dispatch_kernel(src_ref, nu_ref, h_hbm, o_ref, buf, sem):
    i = pl.program_id(0)
    tm = buf.shape[0]

    @pl.when(i < nu_ref[0])
    def _():
        base = i * tm

        def issue(r, c):
            _row_copy(h_hbm, src_ref[base + r], buf, r, sem).start()
            return c

        def wait(r, c):
            _row_copy(h_hbm, 0, buf, r, sem).wait()
            return c

        lax.fori_loop(0, tm, issue, 0)
        lax.fori_loop(0, tm, wait, 0)
        o_ref[...] = buf[...].astype(o_ref.dtype)

    @pl.when(i >= nu_ref[0])
    def _():
        o_ref[...] = jnp.zeros(o_ref.shape, o_ref.dtype)


def _dispatch(h, src, nu):
    tm = TM_DISPATCH
    return pl.pallas_call(
        _dispatch_kernel,
        grid_spec=pltpu.PrefetchScalarGridSpec(
            num_scalar_prefetch=2,
            grid=(N_SORT_PAD // tm,),
            in_specs=[pl.BlockSpec(memory_space=pl.ANY)],
            out_specs=pl.BlockSpec((tm, D_MODEL), lambda i, s_r, nu_r: (i, 0)),
            scratch_shapes=[pltpu.VMEM((tm, D_MODEL), F32), pltpu.SemaphoreType.DMA(())]),
        out_shape=jax.ShapeDtypeStruct((N_SORT_PAD, D_MODEL), BF16),
        compiler_params=_cparams(("arbitrary",)),
        name="moe_dispatch",
    )(src, nu, h)


def _combine_kernel(p0_ref, p1_ref, ys_hbm, wt_ref, x_ref, gate_ref, g_ref, o_ref, buf0, buf1, sem):
    i = pl.program_id(0)
    tm = buf0.shape[0]
    base = i * tm

    def issue(r, c):
        _row_copy(ys_hbm, p0_ref[base + r], buf0, r, sem).start()
        _row_copy(ys_hbm, p1_ref[base + r], buf1, r, sem).start()
        return c

    def wait(r, c):
        _row_copy(ys_hbm, 0, buf0, r, sem).wait()
        _row_copy(ys_hbm, 0, buf1, r, sem).wait()
        return c

    lax.fori_loop(0, tm, issue, 0)
    lax.fori_loop(0, tm, wait, 0)
    wt = wt_ref[...]
    y = wt[:, 0:1] * buf0[...] + wt[:, 1:2] * buf1[...]
    o_ref[...] = _post_residual(x_ref[...], y, gate_ref[0], g_ref[...])


def _combine(ys, pos0, pos1, wt, x, gate, g):
    tm = TM_COMBINE
    per_b = SEQ // tm
    return pl.pallas_call(
        _combine_kernel,
        grid_spec=pltpu.PrefetchScalarGridSpec(
            num_scalar_prefetch=2,
            grid=(TOKENS // tm,),
            in_specs=[pl.BlockSpec(memory_space=pl.ANY),
                      pl.BlockSpec((tm, TOP_K), lambda i, a, b: (i, 0)),
                      pl.BlockSpec((tm, D_MODEL), lambda i, a, b: (i, 0)),
                      pl.BlockSpec((1, 1, D_MODEL), lambda i, a, b: (i // per_b, 0, 0)),
                      pl.BlockSpec((1, D_MODEL), lambda i, a, b: (0, 0))],
            out_specs=pl.BlockSpec((tm, D_MODEL), lambda i, a, b: (i, 0)),
            scratch_shapes=[pltpu.VMEM((tm, D_MODEL), F32), pltpu.VMEM((tm, D_MODEL), F32),
                            pltpu.SemaphoreType.DMA(())]),
        out_shape=jax.ShapeDtypeStruct((TOKENS, D_MODEL), F32),
        compiler_params=_cparams(("arbitrary",)),
        name="moe_combine",
    )(pos0, pos1, ys, wt, x, gate, g.reshape(1, D_MODEL))


def _routing_tables(top_i, rank, counts):
    tm = TM_MOE
    n_tiles = N_SORT_PAD // tm
    experts = jnp.arange(N_EXPERTS, dtype=jnp.int32)
    padded = ((counts + tm - 1) // tm) * tm
    ends = jnp.cumsum(padded)
    offs = ends - padded
    group_start = jnp.sum((top_i[:, :, None] == experts) * offs, axis=-1)
    pos = (group_start + rank).astype(jnp.int32)
    tok = jnp.tile(jnp.arange(TOKENS, dtype=jnp.int32), TOP_K)
    src = jnp.zeros((N_SORT_PAD,), jnp.int32).at[pos.reshape(-1)].set(tok)
    tile_start = jnp.arange(n_tiles, dtype=jnp.int32) * tm
    te = jnp.sum((tile_start[:, None] >= ends[None, :]).astype(jnp.int32), axis=1)
    te = jnp.minimum(te, N_EXPERTS - 1)
    nu = (ends[-1] // tm).astype(jnp.int32).reshape(1)
    return src, te, nu, pos[0], pos[1]


def _finer_tiles(te, nu, factor):
    return jnp.repeat(te, factor), nu * factor


def _pad_w_in(w):
    lat = MLA_Q_RANK + MLA_KV_RANK + MLA_ROPE
    pad = jnp.zeros((D_MODEL, 1024 - lat), w.dtype)
    return jnp.concatenate([w[:, :lat], pad, w[:, lat:]], axis=1).astype(BF16)


def _pad_w_q_up(w):
    w = w.reshape(MLA_Q_RANK, MLA_HEADS, MLA_NOPE + MLA_ROPE)
    pad = jnp.zeros((MLA_Q_RANK, MLA_HEADS, MLA_QK_PAD - MLA_NOPE - MLA_ROPE), w.dtype)
    return jnp.concatenate([w, pad], axis=2).reshape(MLA_Q_RANK, MLA_HEADS * MLA_QK_PAD).astype(BF16)


def _perm_w_kv_up(w):
    w = w.reshape(MLA_KV_RANK, MLA_HEADS, MLA_NOPE + MLA_V)
    k = w[:, :, :MLA_NOPE].reshape(MLA_KV_RANK, MLA_HEADS * MLA_NOPE)
    v = w[:, :, MLA_NOPE:].reshape(MLA_KV_RANK, MLA_HEADS * MLA_V)
    return jnp.concatenate([k, v], axis=1).astype(BF16)


def kernel(x, c, positions, ada_w, ada_b, norm_g, w_in, q_norm_g, kv_norm_g, w_q_up, w_kv_up, diff_lambda,
           diff_subln_g, w_branch_a, w_branch_b, b_gate, w_out, dense_w_gu, dense_w_down, router_w, moe_w_gu,
           moe_w_down):
    assert x.shape == (BATCH, SEQ, D_MODEL) and c.shape == (BATCH, D_MODEL)
    assert w_in.shape == (DEPTH, D_MODEL, 8000)
    xt = x.reshape(TOKENS, D_MODEL)
    cos, sin = _rope_tables(positions)
    mod = _adaln(c, ada_w, ada_b)
    dense_tiles = TOKENS // TM_DENSE_GU
    te_dense = jnp.zeros((dense_tiles,), jnp.int32)
    nu_dense = jnp.full((1,), dense_tiles, jnp.int32)

    for l in range(DEPTH):
        shift, scale, gate = (mod[l, 0, :, :, i * D_MODEL:(i + 1) * D_MODEL] for i in range(3))
        h = _norm_mod(xt, norm_g[l, 0], scale, shift)
        proj = _in_proj(h, _pad_w_in(w_in[l]), cos, sin)
        qa, ka, va = _mla_up(proj, q_norm_g[l], kv_norm_g[l], _pad_w_q_up(w_q_up[l]), _perm_w_kv_up(w_kv_up[l]),
                             cos, sin)
        ya = _mla_attn(qa, ka, va)
        yb = _diff_attn(proj, diff_lambda[l], diff_subln_g[l], l)
        xt = _merge_out(ya, yb, proj, b_gate[l], w_branch_a[l].astype(BF16), w_branch_b[l].astype(BF16),
                        w_out[l].astype(BF16), xt, gate, norm_g[l, 1])

        shift, scale, gate = (mod[l, 1, :, :, i * D_MODEL:(i + 1) * D_MODEL] for i in range(3))
        if l % 2 == 0:
            h = _norm_mod(xt, norm_g[l, 2], scale, shift)
            act = _grouped_gu(h, dense_w_gu[l // 2][None], te_dense, nu_dense, TM_DENSE_GU, TN_DENSE_GU)
            te_d, nu_d = _finer_tiles(te_dense, nu_dense, TM_DENSE_GU // TM_DENSE_DOWN)
            y = _grouped_down(act, dense_w_down[l // 2][None], te_d, nu_d, TM_DENSE_DOWN, TN_DENSE_DOWN)
            xt = _post(y, xt, gate, norm_g[l, 3])
        else:
            h, top_i, top_w, rank, cnt = _norm_mod_route(xt, norm_g[l, 2], scale, shift, router_w[l // 2])
            src, te, nu, pos0, pos1 = _routing_tables(top_i, rank, cnt[:, 0].astype(jnp.int32))
            xs = _dispatch(h, src, _finer_tiles(te, nu, TM_MOE // TM_DISPATCH)[1])
            act = _grouped_gu(xs, moe_w_gu[l // 2], te, nu, TM_MOE, TN_MOE_GU)
            te_d, nu_d = _finer_tiles(te, nu, TM_MOE // TM_MOE_DOWN)
            ys = _grouped_down(act, moe_w_down[l // 2], te_d, nu_d, TM_MOE_DOWN, TN_MOE_DOWN)
            xt = _combine(ys, pos0, pos1, top_w.T, xt, gate, norm_g[l, 3])
    return xt.reshape(BATCH, SEQ, D_MODEL)
```

```python
import functools
import math

import jax
import jax.numpy as jnp
import numpy as np
from jax import lax
from jax.experimental import pallas as pl
from jax.experimental.pallas import tpu as pltpu

F32 = jnp.float32
BF16 = jnp.bfloat16

D_MODEL = 2048
BATCH = 2
SEQ = 4096
TOKENS = BATCH * SEQ
DEPTH = 2
CHUNK = 64
ROPE_THETA = 10000.0
EPS = 1e-6
NEG_INF = -1e30

MLA_HEADS = 8
MLA_Q_RANK = 512
MLA_KV_RANK = 256
MLA_NOPE = 128
MLA_ROPE = 64
MLA_V = 128
MLA_QK_PAD = 256

DIFF_HEADS = 8
DIFF_HEAD_DIM = 64

D_FF_DENSE = 5632
N_EXPERTS = 8
TOP_K = 2
D_FF_EXPERT = 7168

D_IN_PAD = 8192

LANES = 128
V7X_VMEM_LIMIT = 56 * 1024 * 1024

TM_NORM = 512
TM_IN = 1024
TN_IN = 1024
TM_UP = 512
TQ = 1024
TK = 512
TM_MERGE = 256
TM_DENSE = 512
TN_DENSE_GU = 512
TN_DENSE_DOWN = 512
TM_MOE = 256
TM_DISPATCH = 256
TN_MOE_GU = 1024
TN_MOE_DOWN = 512
N_SORT_PAD = TOP_K * TOKENS + N_EXPERTS * TM_MOE
TM_COMBINE = 256


def _cparams(sem, vmem=None):
    return pltpu.CompilerParams(dimension_semantics=sem, vmem_limit_bytes=vmem)


def _nt_dot(a, b):
    return lax.dot_general(a, b, (((1,), (1,)), ((), ())), preferred_element_type=F32)


def _rms(y):
    return y * lax.rsqrt(jnp.mean(y * y, axis=-1, keepdims=True) + EPS)


def _rope_lanes(blk, cos, sin_signed, first_half):
    rot = jnp.where(first_half, pltpu.roll(blk, 96, 1), pltpu.roll(blk, 32, 1))
    return blk * cos + rot * sin_signed


def _first_half_mask(rows):
    lane = lax.broadcasted_iota(jnp.int32, (rows, LANES), 1)
    return (lane % 64) < 32


def _rope_tables_kernel(pos_ref, freq_ref, sign_ref, cos_ref, sin_ref):
    ang = pos_ref[...].astype(F32) * freq_ref[...]
    cos_ref[...] = jnp.cos(ang)
    sin_ref[...] = jnp.sin(ang) * sign_ref[...]


def _rope_tables(positions):
    half = MLA_ROPE // 2
    inv_freq = ROPE_THETA ** (-jnp.arange(0, MLA_ROPE, 2, dtype=F32) / MLA_ROPE)
    freq = jnp.tile(inv_freq, LANES // half).reshape(1, LANES)
    sign = jnp.tile(jnp.concatenate([-jnp.ones((half,), F32), jnp.ones((half,), F32)]), LANES // MLA_ROPE)
    sign = sign.reshape(1, LANES)
    pos = positions.reshape(TOKENS, 1)
    tm = 1024
    return pl.pallas_call(
        _rope_tables_kernel,
        grid=(TOKENS // tm,),
        in_specs=[pl.BlockSpec((tm, 1), lambda i: (i, 0)),
                  pl.BlockSpec((1, LANES), lambda i: (0, 0)),
                  pl.BlockSpec((1, LANES), lambda i: (0, 0))],
        out_specs=[pl.BlockSpec((tm, LANES), lambda i: (i, 0)),
                   pl.BlockSpec((tm, LANES), lambda i: (i, 0))],
        out_shape=[jax.ShapeDtypeStruct((TOKENS, LANES), F32)] * 2,
        compiler_params=_cparams(("parallel",)),
        name="rope_tables",
    )(pos, freq, sign)


def _adaln_kernel(c_ref, w_ref, b_ref, o_ref):
    c = c_ref[...]
    cs = (c * jax.nn.sigmoid(c)).astype(BF16)
    o_ref[0] = jnp.dot(cs, w_ref[0].astype(BF16), preferred_element_type=F32) + b_ref[0]


def _adaln(c, ada_w, ada_b):
    n_mod = DEPTH * 2
    w = ada_w.reshape(n_mod, D_MODEL, 3 * D_MODEL)
    b = ada_b.reshape(n_mod, 1, 3 * D_MODEL)
    c8 = jnp.zeros((8, D_MODEL), F32).at[:BATCH].set(c)
    tn = 1024
    out = pl.pallas_call(
        _adaln_kernel,
        grid=(n_mod, 3 * D_MODEL // tn),
        in_specs=[pl.BlockSpec((8, D_MODEL), lambda m, n: (0, 0)),
                  pl.BlockSpec((1, D_MODEL, tn), lambda m, n: (m, 0, n)),
                  pl.BlockSpec((1, 1, tn), lambda m, n: (m, 0, n))],
        out_specs=pl.BlockSpec((1, 8, tn), lambda m, n: (m, 0, n)),
        out_shape=jax.ShapeDtypeStruct((n_mod, 8, 3 * D_MODEL), F32),
        compiler_params=_cparams(("parallel", "parallel"), V7X_VMEM_LIMIT),
        name="adaln",
    )(c8, w, b)
    mod = out[:, :BATCH, :].reshape(DEPTH, 2, BATCH, 1, 3 * D_MODEL)
    return mod


def _norm_mod_kernel(x_ref, g_ref, sc_ref, sh_ref, h_ref):
    h = _rms(x_ref[...]) * g_ref[...] * (1.0 + sc_ref[0]) + sh_ref[0]
    h_ref[...] = h.astype(h_ref.dtype)


def _norm_mod_route_kernel(x_ref, g_ref, sc_ref, sh_ref, rwh_ref, rwl_ref, h_ref, idx_ref, wt_ref, rank_ref,
                           cnt_ref, cnt_s):
    @pl.when(pl.program_id(0) == 0)
    def _():
        cnt_s[...] = jnp.zeros(cnt_s.shape, F32)

    h = _rms(x_ref[...]) * g_ref[...] * (1.0 + sc_ref[0]) + sh_ref[0]
    h_ref[...] = h
    h_hi = h.astype(BF16)
    h_lo = (h - h_hi.astype(F32)).astype(BF16)
    logits = (jnp.dot(h_hi, rwh_ref[...], preferred_element_type=F32)
              + jnp.dot(h_hi, rwl_ref[...], preferred_element_type=F32)
              + jnp.dot(h_lo, rwh_ref[...], preferred_element_type=F32))
    lt = logits.T[:N_EXPERTS, :]
    eid = lax.broadcasted_iota(jnp.int32, lt.shape, 0)
    m1 = jnp.max(lt, axis=0, keepdims=True)
    i1 = jnp.min(jnp.where(lt == m1, eid, N_EXPERTS), axis=0, keepdims=True)
    lt2 = jnp.where(eid == i1, -jnp.inf, lt)
    m2 = jnp.max(lt2, axis=0, keepdims=True)
    i2 = jnp.min(jnp.where(lt2 == m2, eid, N_EXPERTS), axis=0, keepdims=True)
    e2 = jnp.exp(m2 - m1)
    w1 = 1.0 / (1.0 + e2)
    idx_ref[0:1, :] = i1
    idx_ref[1:2, :] = i2
    wt_ref[0:1, :] = w1
    wt_ref[1:2, :] = e2 * w1
    tm = lt.shape[1]
    chosen = jnp.logical_or(eid == i1, eid == i2)
    upper = (lax.broadcasted_iota(jnp.int32, (tm, tm), 0) < lax.broadcasted_iota(jnp.int32, (tm, tm), 1))
    before = jnp.dot(chosen.astype(BF16), upper.astype(BF16), preferred_element_type=F32) + cnt_s[:, 0:1]
    rank_ref[0:1, :] = jnp.sum(jnp.where(eid == i1, before, 0.0), axis=0, keepdims=True).astype(jnp.int32)
    rank_ref[1:2, :] = jnp.sum(jnp.where(eid == i2, before, 0.0), axis=0, keepdims=True).astype(jnp.int32)
    cnt_s[...] = cnt_s[...] + jnp.sum(chosen.astype(F32), axis=1, keepdims=True)
    cnt_ref[...] = cnt_s[...]


def _mod_specs(tm):
    per_b = SEQ // tm
    return [pl.BlockSpec((tm, D_MODEL), lambda i: (i, 0)),
            pl.BlockSpec((1, D_MODEL), lambda i: (0, 0)),
            pl.BlockSpec((1, 1, D_MODEL), lambda i: (i // per_b, 0, 0)),
            pl.BlockSpec((1, 1, D_MODEL), lambda i: (i // per_b, 0, 0))]


def _norm_mod(x, g, scale, shift):
    tm = TM_NORM
    return pl.pallas_call(
        _norm_mod_kernel,
        grid=(TOKENS // tm,),
        in_specs=_mod_specs(tm),
        out_specs=pl.BlockSpec((tm, D_MODEL), lambda i: (i, 0)),
        out_shape=jax.ShapeDtypeStruct((TOKENS, D_MODEL), BF16),
        compiler_params=_cparams(("parallel",)),
        name="norm_mod",
    )(x, g.reshape(1, D_MODEL), scale, shift)


def _norm_mod_route(x, g, scale, shift, router_w):
    tm = TM_NORM
    rw = jnp.zeros((D_MODEL, LANES), F32).at[:, :N_EXPERTS].set(router_w)
    rw_hi = rw.astype(BF16)
    rw_lo = (rw - rw_hi.astype(F32)).astype(BF16)
    return pl.pallas_call(
        _norm_mod_route_kernel,
        grid=(TOKENS // tm,),
        in_specs=_mod_specs(tm) + [pl.BlockSpec((D_MODEL, LANES), lambda i: (0, 0)),
                                   pl.BlockSpec((D_MODEL, LANES), lambda i: (0, 0))],
        out_specs=[pl.BlockSpec((tm, D_MODEL), lambda i: (i, 0)),
                   pl.BlockSpec((TOP_K, tm), lambda i: (0, i)),
                   pl.BlockSpec((TOP_K, tm), lambda i: (0, i)),
                   pl.BlockSpec((TOP_K, tm), lambda i: (0, i)),
                   pl.BlockSpec((N_EXPERTS, LANES), lambda i: (0, 0))],
        out_shape=[jax.ShapeDtypeStruct((TOKENS, D_MODEL), F32),
                   jax.ShapeDtypeStruct((TOP_K, TOKENS), jnp.int32),
                   jax.ShapeDtypeStruct((TOP_K, TOKENS), F32),
                   jax.ShapeDtypeStruct((TOP_K, TOKENS), jnp.int32),
                   jax.ShapeDtypeStruct((N_EXPERTS, LANES), F32)],
        scratch_shapes=[pltpu.VMEM((N_EXPERTS, LANES), F32)],
        compiler_params=_cparams(("arbitrary",)),
        name="norm_mod_route",
    )(x, g.reshape(1, D_MODEL), scale, shift, rw_hi, rw_lo)


def _in_proj_kernel(h_ref, w_ref, cos_ref, sin_ref, o_ref):
    n = pl.program_id(0)
    acc = jnp.dot(h_ref[...], w_ref[...], preferred_element_type=F32)
    tm, tn = acc.shape

    def rope_cols(lo, hi, mult):
        fh = _first_half_mask(tm)
        cos, sin = cos_ref[...], sin_ref[...]
        for c in range(lo // LANES, hi // LANES):
            blk = acc[:, c * LANES:(c + 1) * LANES]
            o_ref[:, c * LANES:(c + 1) * LANES] = (_rope_lanes(blk, cos, sin, fh) * mult).astype(o_ref.dtype)

    @pl.when(n == 0)
    def _():
        lat = MLA_Q_RANK + MLA_KV_RANK
        o_ref[:, :lat] = acc[:, :lat].astype(o_ref.dtype)
        rope_cols(lat, tn, 1.0)

    @pl.when(n == 1)
    def _():
        rope_cols(0, tn, DIFF_HEAD_DIM ** -0.5)

    @pl.when(n == 2)
    def _():
        rope_cols(0, tn, 1.0)

    @pl.when(n >= 3)
    def _():
        o_ref[...] = acc.astype(o_ref.dtype)


def _in_proj(h, w_in_p, cos, sin):
    tm, tn = TM_IN, TN_IN
    return pl.pallas_call(
        _in_proj_kernel,
        grid=(D_IN_PAD // tn, TOKENS // tm),
        in_specs=[pl.BlockSpec((tm, D_MODEL), lambda n, m: (m, 0)),
                  pl.BlockSpec((D_MODEL, tn), lambda n, m: (0, n)),
                  pl.BlockSpec((tm, LANES), lambda n, m: (m, 0)),
                  pl.BlockSpec((tm, LANES), lambda n, m: (m, 0))],
        out_specs=pl.BlockSpec((tm, tn), lambda n, m: (m, n)),
        out_shape=jax.ShapeDtypeStruct((TOKENS, D_IN_PAD), BF16),
        compiler_params=_cparams(("parallel", "parallel"), V7X_VMEM_LIMIT),
        name="in_proj",
    )(h, w_in_p, cos, sin)


def _mla_up_kernel(lat_ref, qg_ref, kvg_ref, wq_ref, wkv_ref, cos_ref, sin_ref, q_ref, k_ref, v_ref):
    tm = lat_ref.shape[0]
    scale = (MLA_NOPE + MLA_ROPE) ** -0.5
    q_lat = lat_ref[:, :MLA_Q_RANK].astype(F32)
    kv_lat = lat_ref[:, MLA_Q_RANK:MLA_Q_RANK + MLA_KV_RANK].astype(F32)
    k_pe = lat_ref[:, MLA_Q_RANK + MLA_KV_RANK:MLA_Q_RANK + MLA_KV_RANK + LANES]
    qn = (_rms(q_lat) * qg_ref[...]).astype(BF16)
    kvn = (_rms(kv_lat) * kvg_ref[...]).astype(BF16)
    q = jnp.dot(qn, wq_ref[...], preferred_element_type=F32)
    kv = jnp.dot(kvn, wkv_ref[...], preferred_element_type=F32)
    fh = _first_half_mask(tm)
    cos, sin = cos_ref[...], sin_ref[...]
    for h in range(MLA_HEADS):
        c0 = h * MLA_QK_PAD
        q_ref[:, c0:c0 + LANES] = (q[:, c0:c0 + LANES] * scale).astype(BF16)
        q_pe = _rope_lanes(q[:, c0 + LANES:c0 + 2 * LANES], cos, sin, fh)
        q_ref[:, c0 + LANES:c0 + 2 * LANES] = (q_pe * scale).astype(BF16)
        k_ref[:, c0:c0 + LANES] = kv[:, h * LANES:(h + 1) * LANES].astype(BF16)
        k_ref[:, c0 + LANES:c0 + 2 * LANES] = k_pe
    v_ref[...] = kv[:, MLA_HEADS * MLA_NOPE:].astype(BF16)


def _mla_up(proj, q_norm_g, kv_norm_g, wq_p, wkv_p, cos, sin):
    tm = TM_UP
    lat_w = 1024
    hq = MLA_HEADS * MLA_QK_PAD
    hv = MLA_HEADS * MLA_V
    const = lambda i: (0, 0)
    return pl.pallas_call(
        _mla_up_kernel,
        grid=(TOKENS // tm,),
        in_specs=[pl.BlockSpec((tm, lat_w), lambda i: (i, 0)),
                  pl.BlockSpec((1, MLA_Q_RANK), const),
                  pl.BlockSpec((1, MLA_KV_RANK), const),
                  pl.BlockSpec((MLA_Q_RANK, hq), const),
                  pl.BlockSpec((MLA_KV_RANK, 2 * hv), const),
                  pl.BlockSpec((tm, LANES), lambda i: (i, 0)),
                  pl.BlockSpec((tm, LANES), lambda i: (i, 0))],
        out_specs=[pl.BlockSpec((tm, hq), lambda i: (i, 0)),
                   pl.BlockSpec((tm, hq), lambda i: (i, 0)),
                   pl.BlockSpec((tm, hv), lambda i: (i, 0))],
        out_shape=[jax.ShapeDtypeStruct((TOKENS, hq), BF16),
                   jax.ShapeDtypeStruct((TOKENS, hq), BF16),
                   jax.ShapeDtypeStruct((TOKENS, hv), BF16)],
        compiler_params=_cparams(("parallel",), V7X_VMEM_LIMIT),
        name="mla_up",
    )(proj, q_norm_g.reshape(1, -1), kv_norm_g.reshape(1, -1), wq_p, wkv_p, cos, sin)


def _masked_scores(s, diag):
    tq, tk = s.shape
    qc = lax.broadcasted_iota(jnp.int32, (tq, tk), 0) // CHUNK
    kc = (lax.broadcasted_iota(jnp.int32, (tq, tk), 1) + diag * tk) // CHUNK
    return jnp.where(kc <= qc, s, NEG_INF)


def _key_rows(j):
    return pl.ds(pl.multiple_of(j * TK, TK), TK)


def _softmax_pv(s, vaug, m_s, acc_s):
    tq, tk = s.shape
    m_prev = m_s[...]
    m_new = jnp.maximum(m_prev, jnp.max(s, axis=1, keepdims=True))
    alpha = jnp.exp(m_prev - m_new)
    p = jnp.exp(s - jnp.tile(m_new, (1, tk // LANES)))
    pv = jnp.dot(p.astype(BF16), vaug, preferred_element_type=F32)
    acc_s[...] = jnp.tile(alpha, (1, 2)) * acc_s[...] + pv
    m_s[...] = m_new


def _attn_sweep(qi, scores, vaug_s, states):
    n_diag = TQ // TK
    n_full = qi * n_diag

    def consume(ss, j):
        vaug = vaug_s[_key_rows(j), :]
        for s, (m_s, acc_s) in zip(ss, states):
            _softmax_pv(s, vaug, m_s, acc_s)

    pending, idx = scores(n_full, 0), n_full
    for d in range(1, n_diag):
        nxt = scores(n_full + d, d)
        consume(pending, idx)
        pending, idx = nxt, n_full + d

    def body(j, carry):
        nxt = scores(j, None)
        consume(*carry)
        return nxt, j

    consume(*lax.fori_loop(0, n_full, body, (pending, idx)))


def _init_attn_state(v_ref, vaug_s, states):
    @pl.when(pl.program_id(2) == 0)
    def _():
        vaug_s[:, :LANES] = v_ref[...]
        vaug_s[:, LANES:] = jnp.ones((vaug_s.shape[0], LANES), vaug_s.dtype)

    for m_s, acc_s in states:
        m_s[...] = jnp.full(m_s.shape, -jnp.inf, F32)
        acc_s[...] = jnp.zeros(acc_s.shape, F32)


def _attn_scratch(n_states):
    return ([pltpu.VMEM((SEQ, 2 * LANES), BF16)]
            + [pltpu.VMEM((TQ, LANES), F32), pltpu.VMEM((TQ, 2 * LANES), F32)] * n_states)


def _normalised(acc_s):
    return acc_s[:, :LANES] * (1.0 / acc_s[:, LANES:])


def _mla_attn_kernel(q_ref, k_ref, v_ref, o_ref, vaug_s, m_s, acc_s):
    tq = q_ref.shape[0]
    qi = pl.program_id(2)
    states = ((m_s, acc_s),)
    _init_attn_state(v_ref, vaug_s, states)
    q = q_ref[...]

    def scores(j, diag):
        s = _nt_dot(q, k_ref[_key_rows(j), :])
        return (s if diag is None else _masked_scores(s, diag),)

    _attn_sweep(qi, scores, vaug_s, states)
    o_ref[...] = _normalised(acc_s).astype(o_ref.dtype)


def _mla_attn(qa, ka, va):
    tq = TQ
    nq = SEQ // tq
    return pl.pallas_call(
        _mla_attn_kernel,
        grid=(BATCH, MLA_HEADS, nq),
        in_specs=[pl.BlockSpec((tq, MLA_QK_PAD), lambda b, h, i: (b * nq + i, h)),
                  pl.BlockSpec((SEQ, MLA_QK_PAD), lambda b, h, i: (b, h)),
                  pl.BlockSpec((SEQ, MLA_V), lambda b, h, i: (b, h))],
        out_specs=pl.BlockSpec((tq, MLA_V), lambda b, h, i: (b * nq + i, h)),
        out_shape=jax.ShapeDtypeStruct((TOKENS, MLA_HEADS * MLA_V), BF16),
        scratch_shapes=_attn_scratch(1),
        compiler_params=_cparams(("parallel", "parallel", "arbitrary"), V7X_VMEM_LIMIT),
        name="mla_attn",
    )(qa, ka, va)


def _diff_attn_kernel(q_ref, k_ref, v_ref, lam_ref, g_ref, o_ref, vaug_s, m1, a1, m2, a2, *, lam_init):
    tq = q_ref.shape[0]
    qi = pl.program_id(2)
    states = ((m1, a1), (m2, a2))
    _init_attn_state(v_ref, vaug_s, states)
    q = q_ref[...]
    lane = lax.broadcasted_iota(jnp.int32, q.shape, 1)
    zero = jnp.zeros_like(q)
    q1 = jnp.where(lane < DIFF_HEAD_DIM, q, zero)
    q2 = jnp.where(lane >= DIFF_HEAD_DIM, q, zero)

    def scores(j, diag):
        k = k_ref[_key_rows(j), :]
        s1 = _nt_dot(q1, k)
        s2 = _nt_dot(q2, k)
        if diag is not None:
            s1, s2 = _masked_scores(s1, diag), _masked_scores(s2, diag)
        return (s1, s2)

    _attn_sweep(qi, scores, vaug_s, states)
    lp = lam_ref[...]
    lam = (jnp.exp(jnp.sum(lp[0:1] * lp[1:2], axis=1, keepdims=True))
           - jnp.exp(jnp.sum(lp[2:3] * lp[3:4], axis=1, keepdims=True)) + lam_init)
    od = _normalised(a1) - lam * _normalised(a2)
    o_ref[...] = (_rms(od) * g_ref[...] * (1.0 - lam_init)).astype(o_ref.dtype)


def _diff_attn(proj, diff_lambda, subln_g, layer):
    tq = TQ
    nq = SEQ // tq
    hd = 2 * DIFF_HEAD_DIM
    q0, k0, v0 = 1024 // hd, 2048 // hd, 3072 // hd
    lam_init = 0.8 - 0.6 * math.exp(-0.3 * layer)
    return pl.pallas_call(
        functools.partial(_diff_attn_kernel, lam_init=lam_init),
        grid=(BATCH, DIFF_HEADS, nq),
        in_specs=[pl.BlockSpec((tq, hd), lambda b, h, i: (b * nq + i, q0 + h)),
                  pl.BlockSpec((SEQ, hd), lambda b, h, i: (b, k0 + h)),
                  pl.BlockSpec((SEQ, hd), lambda b, h, i: (b, v0 + h)),
                  pl.BlockSpec((4, DIFF_HEAD_DIM), lambda b, h, i: (0, 0)),
                  pl.BlockSpec((1, hd), lambda b, h, i: (0, 0))],
        out_specs=pl.BlockSpec((tq, hd), lambda b, h, i: (b * nq + i, h)),
        out_shape=jax.ShapeDtypeStruct((TOKENS, DIFF_HEADS * hd), BF16),
        scratch_shapes=_attn_scratch(2),
        compiler_params=_cparams(("parallel", "parallel", "arbitrary"), V7X_VMEM_LIMIT),
        name="diff_attn",
    )(proj, proj, proj, diff_lambda, subln_g.reshape(1, hd))


def _post_residual(x, y, gate, g):
    return x + gate * (_rms(y) * g)


def _merge_out_kernel(ya_ref, yb_ref, ga_ref, gb_ref, bg_ref, wa_ref, wb_ref, wo_ref, x_ref, gate_ref, g_ref,
                      o_ref):
    a = jnp.dot(ya_ref[...], wa_ref[...], preferred_element_type=F32)
    b = jnp.dot(yb_ref[...], wb_ref[...], preferred_element_type=F32)
    bg = bg_ref[...]
    merged = (jax.nn.sigmoid(ga_ref[...].astype(F32) + bg[0:1]) * a
              + jax.nn.sigmoid(gb_ref[...].astype(F32) + bg[1:2]) * b)
    y = jnp.dot(merged.astype(BF16), wo_ref[...], preferred_element_type=F32)
    o_ref[...] = _post_residual(x_ref[...], y, gate_ref[0], g_ref[...])


def _resident(shape):
    return pl.BlockSpec(shape, lambda i: (0,) * len(shape), pipeline_mode=pl.Buffered(1))


def _merge_out(ya, yb, proj, b_gate, wa, wb, wo, x, gate, g):
    tm = TM_MERGE
    per_b = SEQ // tm
    ga_blk, gb_blk = 4096 // D_MODEL, 6144 // D_MODEL
    return pl.pallas_call(
        _merge_out_kernel,
        grid=(TOKENS // tm,),
        in_specs=[pl.BlockSpec((tm, 1024), lambda i: (i, 0)),
                  pl.BlockSpec((tm, 1024), lambda i: (i, 0)),
                  pl.BlockSpec((tm, D_MODEL), lambda i: (i, ga_blk)),
                  pl.BlockSpec((tm, D_MODEL), lambda i: (i, gb_blk)),
                  _resident((2, D_MODEL)),
                  _resident((1024, D_MODEL)),
                  _resident((1024, D_MODEL)),
                  _resident((D_MODEL, D_MODEL)),
                  pl.BlockSpec((tm, D_MODEL), lambda i: (i, 0)),
                  pl.BlockSpec((1, 1, D_MODEL), lambda i: (i // per_b, 0, 0)),
                  _resident((1, D_MODEL))],
        out_specs=pl.BlockSpec((tm, D_MODEL), lambda i: (i, 0)),
        out_shape=jax.ShapeDtypeStruct((TOKENS, D_MODEL), F32),
        compiler_params=_cparams(("parallel",), V7X_VMEM_LIMIT),
        name="merge_out",
    )(ya, yb, proj, proj, b_gate, wa, wb, wo, x, gate, g.reshape(1, D_MODEL))


def _grouped_kernel(t0_ref, nt_ref, x_hbm, *refs, swiglu, tm, tn, n_row_tiles):
    n_w = 2 if swiglu else 1
    w_refs, o_hbm, w_bf16 = refs[:n_w], refs[n_w], refs[n_w + 1:2 * n_w + 1]
    xbuf, obuf, sem_x, sem_o = refs[2 * n_w + 1:]
    n, e = pl.program_id(0), pl.program_id(1)
    t0, nt = t0_ref[e], nt_ref[e]
    cols = pl.ds(pl.multiple_of(n * tn, tn), tn)

    def rows(t):
        return pl.ds(pl.multiple_of(t * tm, tm), tm)

    def x_copy(t, slot):
        return pltpu.make_async_copy(x_hbm.at[rows(t0 + t)], xbuf.at[slot], sem_x.at[slot])

    def o_copy(t, slot):
        return pltpu.make_async_copy(obuf.at[slot], o_hbm.at[rows(t), cols], sem_o.at[slot])

    @pl.when(nt > 0)
    def _():
        x_copy(0, 0).start()
        for w_ref, w_s in zip(w_refs, w_bf16):
            w_s[...] = w_ref[0].astype(BF16)

        def body(t, carry):
            slot = lax.rem(t, 2)
            x_copy(t, slot).wait()

            @pl.when(t + 1 < nt)
            def _():
                x_copy(t + 1, 1 - slot).start()

            @pl.when(t >= 2)
            def _():
                o_copy(t0 + t - 2, slot).wait()

            x = xbuf[slot]
            if swiglu:
                g = jnp.dot(x, w_bf16[0][...], preferred_element_type=F32)
                u = jnp.dot(x, w_bf16[1][...], preferred_element_type=F32)
                obuf[slot] = (g * jax.nn.sigmoid(g) * u).astype(obuf.dtype)
            else:
                obuf[slot] = jnp.dot(x, w_bf16[0][...], preferred_element_type=F32).astype(obuf.dtype)
            o_copy(t0 + t, slot).start()
            return carry

        lax.fori_loop(0, nt, body, 0)

        @pl.when(nt >= 2)
        def _():
            o_copy(t0 + nt - 2, lax.rem(nt, 2)).wait()

        o_copy(t0 + nt - 1, lax.rem(nt + 1, 2)).wait()

    @pl.when(e == pl.num_programs(1) - 1)
    def _():
        obuf[0] = jnp.zeros(obuf.shape[1:], obuf.dtype)

        def fill(t, carry):
            o_copy(t, 0).start()
            o_copy(t, 0).wait()
            return carry

        lax.fori_loop(t0 + nt, n_row_tiles, fill, 0)


def _grouped_matmul(x, w, t0, nt, tm, tn, swiglu, out_dtype, name):
    rows, k = x.shape
    n_experts = w.shape[0]
    n_out = w.shape[-1] // 2 if swiglu else w.shape[-1]
    n_tiles = n_out // tn
    n_w = 2 if swiglu else 1
    w_specs = [pl.BlockSpec((1, k, tn), lambda n, e, a, b, off=i * n_tiles: (e, 0, n + off)) for i in range(n_w)]
    return pl.pallas_call(
        functools.partial(_grouped_kernel, swiglu=swiglu, tm=tm, tn=tn, n_row_tiles=rows // tm),
        grid_spec=pltpu.PrefetchScalarGridSpec(
            num_scalar_prefetch=2,
            grid=(n_tiles, n_experts),
            in_specs=[pl.BlockSpec(memory_space=pl.ANY)] + w_specs,
            out_specs=pl.BlockSpec(memory_space=pl.ANY),
            scratch_shapes=([pltpu.VMEM((k, tn), BF16)] * n_w
                            + [pltpu.VMEM((2, tm, k), x.dtype), pltpu.VMEM((2, tm, tn), out_dtype),
                               pltpu.SemaphoreType.DMA((2,)), pltpu.SemaphoreType.DMA((2,))])),
        out_shape=jax.ShapeDtypeStruct((rows, n_out), out_dtype),
        compiler_params=_cparams(("arbitrary", "arbitrary"), V7X_VMEM_LIMIT),
        name=name,
    )(t0, nt, x, *([w] * n_w))


def _grouped_gu(x, w_gu, t0, nt, tm, tn):
    return _grouped_matmul(x, w_gu, t0, nt, tm, tn, True, BF16, "grouped_gate_up")


def _grouped_down(x, w_down, t0, nt, tm, tn):
    return _grouped_matmul(x, w_down, t0, nt, tm, tn, False, F32, "grouped_down")


def _post_kernel(y_ref, x_ref, gate_ref, g_ref, o_ref):
    o_ref[...] = _post_residual(x_ref[...], y_ref[...], gate_ref[0], g_ref[...])


def _post(y, x, gate, g):
    tm = TM_NORM
    per_b = SEQ // tm
    return pl.pallas_call(
        _post_kernel,
        grid=(TOKENS // tm,),
        in_specs=[pl.BlockSpec((tm, D_MODEL), lambda i: (i, 0)),
                  pl.BlockSpec((tm, D_MODEL), lambda i: (i, 0)),
                  pl.BlockSpec((1, 1, D_MODEL), lambda i: (i // per_b, 0, 0)),
                  pl.BlockSpec((1, D_MODEL), lambda i: (0, 0))],
        out_specs=pl.BlockSpec((tm, D_MODEL), lambda i: (i, 0)),
        out_shape=jax.ShapeDtypeStruct((TOKENS, D_MODEL), F32),
        compiler_params=_cparams(("parallel",), V7X_VMEM_LIMIT),
        name="post_norm_residual",
    )(y, x, gate, g.reshape(1, D_MODEL))


def _row_copy(src_hbm, row, dst, slot, sem):
    return pltpu.make_async_copy(src_hbm.at[pl.ds(row, 1)], dst.at[pl.ds(slot, 1)], sem)


def _dispatch_kernel(src_ref, nu_ref, h_hbm, o_ref, buf, sem):
    i = pl.program_id(0)
    tm = buf.shape[0]

    @pl.when(i < nu_ref[0])
    def _():
        base = i * tm

        def issue(r, c):
            _row_copy(h_hbm, src_ref[base + r], buf, r, sem).start()
            return c

        def wait(r, c):
            _row_copy(h_hbm, 0, buf, r, sem).wait()
            return c

        lax.fori_loop(0, tm, issue, 0)
        lax.fori_loop(0, tm, wait, 0)
        o_ref[...] = buf[...].astype(o_ref.dtype)

    @pl.when(i >= nu_ref[0])
    def _():
        o_ref[...] = jnp.zeros(o_ref.shape, o_ref.dtype)


def _dispatch(h, src, nu):
    tm = TM_DISPATCH
    return pl.pallas_call(
        _dispatch_kernel,
        grid_spec=pltpu.PrefetchScalarGridSpec(
            num_scalar_prefetch=2,
            grid=(N_SORT_PAD // tm,),
            in_specs=[pl.BlockSpec(memory_space=pl.ANY)],
            out_specs=pl.BlockSpec((tm, D_MODEL), lambda i, s_r, nu_r: (i, 0)),
            scratch_shapes=[pltpu.VMEM((tm, D_MODEL), F32), pltpu.SemaphoreType.DMA(())]),
        out_shape=jax.ShapeDtypeStruct((N_SORT_PAD, D_MODEL), BF16),
        compiler_params=_cparams(("arbitrary",)),
        name="moe_dispatch",
    )(src, nu, h)


def _combine_kernel(p0_ref, p1_ref, ys_hbm, wt_ref, x_ref, gate_ref, g_ref, o_ref, buf0, buf1, sem):
    i = pl.program_id(0)
    tm = buf0.shape[0]
    base = i * tm

    def issue(r, c):
        _row_copy(ys_hbm, p0_ref[base + r], buf0, r, sem).start()
        _row_copy(ys_hbm, p1_ref[base + r], buf1, r, sem).start()
        return c

    def wait(r, c):
        _row_copy(ys_hbm, 0, buf0, r, sem).wait()
        _row_copy(ys_hbm, 0, buf1, r, sem).wait()
        return c

    lax.fori_loop(0, tm, issue, 0)
    lax.fori_loop(0, tm, wait, 0)
    wt = wt_ref[...]
    y = wt[:, 0:1] * buf0[...] + wt[:, 1:2] * buf1[...]
    o_ref[...] = _post_residual(x_ref[...], y, gate_ref[0], g_ref[...])


def _combine(ys, pos0, pos1, wt, x, gate, g):
    tm = TM_COMBINE
    per_b = SEQ // tm
    return pl.pallas_call(
        _combine_kernel,
        grid_spec=pltpu.PrefetchScalarGridSpec(
            num_scalar_prefetch=2,
            grid=(TOKENS // tm,),
            in_specs=[pl.BlockSpec(memory_space=pl.ANY),
                      pl.BlockSpec((tm, TOP_K), lambda i, a, b: (i, 0)),
                      pl.BlockSpec((tm, D_MODEL), lambda i, a, b: (i, 0)),
                      pl.BlockSpec((1, 1, D_MODEL), lambda i, a, b: (i // per_b, 0, 0)),
                      pl.BlockSpec((1, D_MODEL), lambda i, a, b: (0, 0))],
            out_specs=pl.BlockSpec((tm, D_MODEL), lambda i, a, b: (i, 0)),
            scratch_shapes=[pltpu.VMEM((tm, D_MODEL), F32), pltpu.VMEM((tm, D_MODEL), F32),
                            pltpu.SemaphoreType.DMA(())]),
        out_shape=jax.ShapeDtypeStruct((TOKENS, D_MODEL), F32),
        compiler_params=_cparams(("arbitrary",)),
        name="moe_combine",
    )(pos0, pos1, ys, wt, x, gate, g.reshape(1, D_MODEL))


def _routing_tables(top_i, rank, counts):
    tm = TM_MOE
    experts = jnp.arange(N_EXPERTS, dtype=jnp.int32)
    nt = (counts + tm - 1) // tm
    t0 = jnp.cumsum(nt) - nt
    group_start = jnp.sum((top_i[:, :, None] == experts) * (t0 * tm), axis=-1)
    pos = (group_start + rank).astype(jnp.int32)
    tok = jnp.tile(jnp.arange(TOKENS, dtype=jnp.int32), TOP_K)
    src = jnp.zeros((N_SORT_PAD,), jnp.int32).at[pos.reshape(-1)].set(tok)
    nu = jnp.sum(nt).astype(jnp.int32).reshape(1)
    return src, t0.astype(jnp.int32), nt.astype(jnp.int32), nu, pos[0], pos[1]


def _pad_w_in(w):
    lat = MLA_Q_RANK + MLA_KV_RANK + MLA_ROPE
    pad = jnp.zeros((D_MODEL, 1024 - lat), w.dtype)
    return jnp.concatenate([w[:, :lat], pad, w[:, lat:]], axis=1).astype(BF16)


def _pad_w_q_up(w):
    w = w.reshape(MLA_Q_RANK, MLA_HEADS, MLA_NOPE + MLA_ROPE)
    pad = jnp.zeros((MLA_Q_RANK, MLA_HEADS, MLA_QK_PAD - MLA_NOPE - MLA_ROPE), w.dtype)
    return jnp.concatenate([w, pad], axis=2).reshape(MLA_Q_RANK, MLA_HEADS * MLA_QK_PAD).astype(BF16)


def _perm_w_kv_up(w):
    w = w.reshape(MLA_KV_RANK, MLA_HEADS, MLA_NOPE + MLA_V)
    k = w[:, :, :MLA_NOPE].reshape(MLA_KV_RANK, MLA_HEADS * MLA_NOPE)
    v = w[:, :, MLA_NOPE:].reshape(MLA_KV_RANK, MLA_HEADS * MLA_V)
    return jnp.concatenate([k, v], axis=1).astype(BF16)


def kernel(x, c, positions, ada_w, ada_b, norm_g, w_in, q_norm_g, kv_norm_g, w_q_up, w_kv_up, diff_lambda,
           diff_subln_g, w_branch_a, w_branch_b, b_gate, w_out, dense_w_gu, dense_w_down, router_w, moe_w_gu,
           moe_w_down):
    assert x.shape == (BATCH, SEQ, D_MODEL) and c.shape == (BATCH, D_MODEL)
    assert w_in.shape == (DEPTH, D_MODEL, 8000)
    xt = x.reshape(TOKENS, D_MODEL)
    cos, sin = _rope_tables(positions)
    mod = _adaln(c, ada_w, ada_b)
    t0_dense = jnp.zeros((1,), jnp.int32)
    nt_dense = jnp.full((1,), TOKENS // TM_DENSE, jnp.int32)

    for l in range(DEPTH):
        shift, scale, gate = (mod[l, 0, :, :, i * D_MODEL:(i + 1) * D_MODEL] for i in range(3))
        h = _norm_mod(xt, norm_g[l, 0], scale, shift)
        proj = _in_proj(h, _pad_w_in(w_in[l]), cos, sin)
        qa, ka, va = _mla_up(proj, q_norm_g[l], kv_norm_g[l], _pad_w_q_up(w_q_up[l]), _perm_w_kv_up(w_kv_up[l]),
                             cos, sin)
        ya = _mla_attn(qa, ka, va)
        yb = _diff_attn(proj, diff_lambda[l], diff_subln_g[l], l)
        xt = _merge_out(ya, yb, proj, b_gate[l], w_branch_a[l].astype(BF16), w_branch_b[l].astype(BF16),
                        w_out[l].astype(BF16), xt, gate, norm_g[l, 1])

        shift, scale, gate = (mod[l, 1, :, :, i * D_MODEL:(i + 1) * D_MODEL] for i in range(3))
        if l % 2 == 0:
            h = _norm_mod(xt, norm_g[l, 2], scale, shift)
            act = _grouped_gu(h, dense_w_gu[l // 2][None], t0_dense, nt_dense, TM_DENSE, TN_DENSE_GU)
            y = _grouped_down(act, dense_w_down[l // 2][None], t0_dense, nt_dense, TM_DENSE, TN_DENSE_DOWN)
            xt = _post(y, xt, gate, norm_g[l, 3])
        else:
            h, top_i, top_w, rank, cnt = _norm_mod_route(xt, norm_g[l, 2], scale, shift, router_w[l // 2])
            src, t0, nt, nu, pos0, pos1 = _routing_tables(top_i, rank, cnt[:, 0].astype(jnp.int32))
            xs = _dispatch(h, src, nu)
            act = _grouped_gu(xs, moe_w_gu[l // 2], t0, nt, TM_MOE, TN_MOE_GU)
            ys = _grouped_down(act, moe_w_down[l // 2], t0, nt, TM_MOE, TN_MOE_DOWN)
            xt = _combine(ys, pos0, pos1, top_w.T, xt, gate, norm_g[l, 3])
    return xt.reshape(BATCH, SEQ, D_MODEL)
```

```python
import functools
import math

import jax
import jax.numpy as jnp
import numpy as np
from jax import lax
from jax.experimental import pallas as pl
from jax.experimental.pallas import tpu as pltpu

F32 = jnp.float32
BF16 = jnp.bfloat16

D_MODEL = 2048
BATCH = 2
SEQ = 4096
TOKENS = BATCH * SEQ
DEPTH = 2
CHUNK = 64
ROPE_THETA = 10000.0
EPS = 1e-6
NEG_INF = -1e30

MLA_HEADS = 8
MLA_Q_RANK = 512
MLA_KV_RANK = 256
MLA_NOPE = 128
MLA_ROPE = 64
MLA_V = 128
MLA_QK_PAD = 256

DIFF_HEADS = 8
DIFF_HEAD_DIM = 64

D_FF_DENSE = 5632
N_EXPERTS = 8
TOP_K = 2
D_FF_EXPERT = 7168

D_IN_PAD = 8192

LANES = 128
N_DMA_QUEUES = 2
V7X_VMEM_LIMIT = 56 * 1024 * 1024
V7X_VMEM_LIMIT_GROUPED = 60 * 1024 * 1024

TM_NORM = 512
TM_IN = 2048
TN_IN = 1024
TM_UP = 512
TQ = 1024
TK = 512
TM_MERGE = 256
TM_DENSE_GU = 1024
TM_DENSE_DOWN = 512
TN_DENSE_GU = 512
TN_DENSE_DOWN = 512
TM_MOE = 512
TM_DISPATCH = 256
TN_MOE_GU = 1024
TN_MOE_DOWN = 512
N_SORT_PAD = TOP_K * TOKENS + N_EXPERTS * TM_MOE
TM_COMBINE = 256


def _cparams(sem, vmem=None):
    return pltpu.CompilerParams(dimension_semantics=sem, vmem_limit_bytes=vmem)


def _nt_dot(a, b):
    return lax.dot_general(a, b, (((1,), (1,)), ((), ())), preferred_element_type=F32)


def _rms(y):
    return y * lax.rsqrt(jnp.mean(y * y, axis=-1, keepdims=True) + EPS)


def _rope_lanes(blk, cos, sin_signed, first_half):
    rot = jnp.where(first_half, pltpu.roll(blk, 96, 1), pltpu.roll(blk, 32, 1))
    return blk * cos + rot * sin_signed


def _first_half_mask(rows):
    lane = lax.broadcasted_iota(jnp.int32, (rows, LANES), 1)
    return (lane % 64) < 32


def _rope_tables_kernel(pos_ref, freq_ref, sign_ref, cos_ref, sin_ref):
    ang = pos_ref[...].astype(F32) * freq_ref[...]
    cos_ref[...] = jnp.cos(ang)
    sin_ref[...] = jnp.sin(ang) * sign_ref[...]


def _rope_tables(positions):
    half = MLA_ROPE // 2
    inv_freq = ROPE_THETA ** (-jnp.arange(0, MLA_ROPE, 2, dtype=F32) / MLA_ROPE)
    freq = jnp.tile(inv_freq, LANES // half).reshape(1, LANES)
    sign = jnp.tile(jnp.concatenate([-jnp.ones((half,), F32), jnp.ones((half,), F32)]), LANES // MLA_ROPE)
    sign = sign.reshape(1, LANES)
    pos = positions.reshape(TOKENS, 1)
    tm = 1024
    return pl.pallas_call(
        _rope_tables_kernel,
        grid=(TOKENS // tm,),
        in_specs=[pl.BlockSpec((tm, 1), lambda i: (i, 0)),
                  pl.BlockSpec((1, LANES), lambda i: (0, 0)),
                  pl.BlockSpec((1, LANES), lambda i: (0, 0))],
        out_specs=[pl.BlockSpec((tm, LANES), lambda i: (i, 0)),
                   pl.BlockSpec((tm, LANES), lambda i: (i, 0))],
        out_shape=[jax.ShapeDtypeStruct((TOKENS, LANES), F32)] * 2,
        compiler_params=_cparams(("parallel",)),
        name="rope_tables",
    )(pos, freq, sign)


def _adaln_kernel(c_ref, w_ref, b_ref, o_ref):
    c = c_ref[...]
    cs = (c * jax.nn.sigmoid(c)).astype(BF16)
    o_ref[0] = jnp.dot(cs, w_ref[0].astype(BF16), preferred_element_type=F32) + b_ref[0]


def _adaln(c, ada_w, ada_b):
    n_mod = DEPTH * 2
    w = ada_w.reshape(n_mod, D_MODEL, 3 * D_MODEL)
    b = ada_b.reshape(n_mod, 1, 3 * D_MODEL)
    c8 = jnp.zeros((8, D_MODEL), F32).at[:BATCH].set(c)
    tn = 1024
    out = pl.pallas_call(
        _adaln_kernel,
        grid=(n_mod, 3 * D_MODEL // tn),
        in_specs=[pl.BlockSpec((8, D_MODEL), lambda m, n: (0, 0)),
                  pl.BlockSpec((1, D_MODEL, tn), lambda m, n: (m, 0, n)),
                  pl.BlockSpec((1, 1, tn), lambda m, n: (m, 0, n))],
        out_specs=pl.BlockSpec((1, 8, tn), lambda m, n: (m, 0, n)),
        out_shape=jax.ShapeDtypeStruct((n_mod, 8, 3 * D_MODEL), F32),
        compiler_params=_cparams(("parallel", "parallel"), V7X_VMEM_LIMIT),
        name="adaln",
    )(c8, w, b)
    mod = out[:, :BATCH, :].reshape(DEPTH, 2, BATCH, 1, 3 * D_MODEL)
    return mod


def _norm_mod_kernel(x_ref, g_ref, sc_ref, sh_ref, h_ref):
    h = _rms(x_ref[...]) * g_ref[...] * (1.0 + sc_ref[0]) + sh_ref[0]
    h_ref[...] = h.astype(h_ref.dtype)


def _norm_mod_route_kernel(x_ref, g_ref, sc_ref, sh_ref, rwh_ref, rwl_ref, h_ref, idx_ref, wt_ref, rank_ref,
                           cnt_ref, cnt_s):
    @pl.when(pl.program_id(0) == 0)
    def _():
        cnt_s[...] = jnp.zeros(cnt_s.shape, F32)

    h = _rms(x_ref[...]) * g_ref[...] * (1.0 + sc_ref[0]) + sh_ref[0]
    h_ref[...] = h
    h_hi = h.astype(BF16)
    h_lo = (h - h_hi.astype(F32)).astype(BF16)
    logits = (jnp.dot(h_hi, rwh_ref[...], preferred_element_type=F32)
              + jnp.dot(h_hi, rwl_ref[...], preferred_element_type=F32)
              + jnp.dot(h_lo, rwh_ref[...], preferred_element_type=F32))
    lt = logits.T[:N_EXPERTS, :]
    eid = lax.broadcasted_iota(jnp.int32, lt.shape, 0)
    m1 = jnp.max(lt, axis=0, keepdims=True)
    i1 = jnp.min(jnp.where(lt == m1, eid, N_EXPERTS), axis=0, keepdims=True)
    lt2 = jnp.where(eid == i1, -jnp.inf, lt)
    m2 = jnp.max(lt2, axis=0, keepdims=True)
    i2 = jnp.min(jnp.where(lt2 == m2, eid, N_EXPERTS), axis=0, keepdims=True)
    e2 = jnp.exp(m2 - m1)
    w1 = 1.0 / (1.0 + e2)
    idx_ref[0:1, :] = i1
    idx_ref[1:2, :] = i2
    wt_ref[0:1, :] = w1
    wt_ref[1:2, :] = e2 * w1
    tm = lt.shape[1]
    chosen = jnp.logical_or(eid == i1, eid == i2)
    upper = (lax.broadcasted_iota(jnp.int32, (tm, tm), 0) < lax.broadcasted_iota(jnp.int32, (tm, tm), 1))
    before = jnp.dot(chosen.astype(BF16), upper.astype(BF16), preferred_element_type=F32) + cnt_s[:, 0:1]
    rank_ref[0:1, :] = jnp.sum(jnp.where(eid == i1, before, 0.0), axis=0, keepdims=True).astype(jnp.int32)
    rank_ref[1:2, :] = jnp.sum(jnp.where(eid == i2, before, 0.0), axis=0, keepdims=True).astype(jnp.int32)
    cnt_s[...] = cnt_s[...] + jnp.sum(chosen.astype(F32), axis=1, keepdims=True)
    cnt_ref[...] = cnt_s[...]


def _mod_specs(tm):
    per_b = SEQ // tm
    return [pl.BlockSpec((tm, D_MODEL), lambda i: (i, 0)),
            pl.BlockSpec((1, D_MODEL), lambda i: (0, 0)),
            pl.BlockSpec((1, 1, D_MODEL), lambda i: (i // per_b, 0, 0)),
            pl.BlockSpec((1, 1, D_MODEL), lambda i: (i // per_b, 0, 0))]


def _norm_mod(x, g, scale, shift):
    tm = TM_NORM
    return pl.pallas_call(
        _norm_mod_kernel,
        grid=(TOKENS // tm,),
        in_specs=_mod_specs(tm),
        out_specs=pl.BlockSpec((tm, D_MODEL), lambda i: (i, 0)),
        out_shape=jax.ShapeDtypeStruct((TOKENS, D_MODEL), BF16),
        compiler_params=_cparams(("parallel",)),
        name="norm_mod",
    )(x, g.reshape(1, D_MODEL), scale, shift)


def _norm_mod_route(x, g, scale, shift, router_w):
    tm = TM_NORM
    rw = jnp.zeros((D_MODEL, LANES), F32).at[:, :N_EXPERTS].set(router_w)
    rw_hi = rw.astype(BF16)
    rw_lo = (rw - rw_hi.astype(F32)).astype(BF16)
    return pl.pallas_call(
        _norm_mod_route_kernel,
        grid=(TOKENS // tm,),
        in_specs=_mod_specs(tm) + [pl.BlockSpec((D_MODEL, LANES), lambda i: (0, 0)),
                                   pl.BlockSpec((D_MODEL, LANES), lambda i: (0, 0))],
        out_specs=[pl.BlockSpec((tm, D_MODEL), lambda i: (i, 0)),
                   pl.BlockSpec((TOP_K, tm), lambda i: (0, i)),
                   pl.BlockSpec((TOP_K, tm), lambda i: (0, i)),
                   pl.BlockSpec((TOP_K, tm), lambda i: (0, i)),
                   pl.BlockSpec((N_EXPERTS, LANES), lambda i: (0, 0))],
        out_shape=[jax.ShapeDtypeStruct((TOKENS, D_MODEL), F32),
                   jax.ShapeDtypeStruct((TOP_K, TOKENS), jnp.int32),
                   jax.ShapeDtypeStruct((TOP_K, TOKENS), F32),
                   jax.ShapeDtypeStruct((TOP_K, TOKENS), jnp.int32),
                   jax.ShapeDtypeStruct((N_EXPERTS, LANES), F32)],
        scratch_shapes=[pltpu.VMEM((N_EXPERTS, LANES), F32)],
        compiler_params=_cparams(("arbitrary",)),
        name="norm_mod_route",
    )(x, g.reshape(1, D_MODEL), scale, shift, rw_hi, rw_lo)


def _in_proj_kernel(h_ref, wlat_ref, wrest_ref, cos_ref, sin_ref, o_ref):
    n = pl.program_id(0)
    tm, tn = o_ref.shape

    def project(w_ref):
        return jnp.dot(h_ref[...], w_ref[...], preferred_element_type=F32)

    def rope_cols(acc, lo, hi, mult):
        fh = _first_half_mask(tm)
        cos, sin = cos_ref[...], sin_ref[...]
        for c in range(lo // LANES, hi // LANES):
            blk = acc[:, c * LANES:(c + 1) * LANES]
            o_ref[:, c * LANES:(c + 1) * LANES] = (_rope_lanes(blk, cos, sin, fh) * mult).astype(o_ref.dtype)

    @pl.when(n == 0)
    def _():
        acc = project(wlat_ref)
        lat = MLA_Q_RANK + MLA_KV_RANK
        o_ref[:, :lat] = acc[:, :lat].astype(o_ref.dtype)
        rope_cols(acc, lat, tn, 1.0)

    @pl.when(n == 1)
    def _():
        rope_cols(project(wrest_ref), 0, tn, DIFF_HEAD_DIM ** -0.5)

    @pl.when(n == 2)
    def _():
        rope_cols(project(wrest_ref), 0, tn, 1.0)

    @pl.when(n >= 3)
    def _():
        o_ref[...] = project(wrest_ref).astype(o_ref.dtype)


def _in_proj(h, w_lat, w_rest, cos, sin):
    tm, tn = TM_IN, TN_IN
    return pl.pallas_call(
        _in_proj_kernel,
        grid=(D_IN_PAD // tn, TOKENS // tm),
        in_specs=[pl.BlockSpec((tm, D_MODEL), lambda n, m: (m, 0)),
                  pl.BlockSpec((D_MODEL, tn), lambda n, m: (0, 0)),
                  pl.BlockSpec((D_MODEL, tn), lambda n, m: (0, jnp.maximum(n - 1, 0))),
                  pl.BlockSpec((tm, LANES), lambda n, m: (m, 0)),
                  pl.BlockSpec((tm, LANES), lambda n, m: (m, 0))],
        out_specs=pl.BlockSpec((tm, tn), lambda n, m: (m, n)),
        out_shape=jax.ShapeDtypeStruct((TOKENS, D_IN_PAD), BF16),
        compiler_params=_cparams(("parallel", "parallel"), V7X_VMEM_LIMIT),
        name="in_proj",
    )(h, w_lat, w_rest, cos, sin)


def _mla_up_kernel(lat_ref, qg_ref, kvg_ref, wq_ref, wkv_ref, cos_ref, sin_ref, q_ref, k_ref, v_ref):
    tm = lat_ref.shape[0]
    scale = (MLA_NOPE + MLA_ROPE) ** -0.5
    q_lat = lat_ref[:, :MLA_Q_RANK].astype(F32)
    kv_lat = lat_ref[:, MLA_Q_RANK:MLA_Q_RANK + MLA_KV_RANK].astype(F32)
    k_pe = lat_ref[:, MLA_Q_RANK + MLA_KV_RANK:MLA_Q_RANK + MLA_KV_RANK + LANES]
    qn = (_rms(q_lat) * qg_ref[...]).astype(BF16)
    kvn = (_rms(kv_lat) * kvg_ref[...]).astype(BF16)
    q = jnp.dot(qn, wq_ref[...], preferred_element_type=F32)
    kv = jnp.dot(kvn, wkv_ref[...], preferred_element_type=F32)
    fh = _first_half_mask(tm)
    cos, sin = cos_ref[...], sin_ref[...]
    for h in range(MLA_HEADS):
        c0 = h * MLA_QK_PAD
        q_ref[:, c0:c0 + LANES] = (q[:, c0:c0 + LANES] * scale).astype(BF16)
        q_pe = _rope_lanes(q[:, c0 + LANES:c0 + 2 * LANES], cos, sin, fh)
        q_ref[:, c0 + LANES:c0 + 2 * LANES] = (q_pe * scale).astype(BF16)
        k_ref[:, c0:c0 + LANES] = kv[:, h * LANES:(h + 1) * LANES].astype(BF16)
        k_ref[:, c0 + LANES:c0 + 2 * LANES] = k_pe
    v_ref[...] = kv[:, MLA_HEADS * MLA_NOPE:].astype(BF16)


def _mla_up(proj, q_norm_g, kv_norm_g, wq_p, wkv_p, cos, sin):
    tm = TM_UP
    lat_w = 1024
    hq = MLA_HEADS * MLA_QK_PAD
    hv = MLA_HEADS * MLA_V
    const = lambda i: (0, 0)
    return pl.pallas_call(
        _mla_up_kernel,
        grid=(TOKENS // tm,),
        in_specs=[pl.BlockSpec((tm, lat_w), lambda i: (i, 0)),
                  pl.BlockSpec((1, MLA_Q_RANK), const),
                  pl.BlockSpec((1, MLA_KV_RANK), const),
                  pl.BlockSpec((MLA_Q_RANK, hq), const),
                  pl.BlockSpec((MLA_KV_RANK, 2 * hv), const),
                  pl.BlockSpec((tm, LANES), lambda i: (i, 0)),
                  pl.BlockSpec((tm, LANES), lambda i: (i, 0))],
        out_specs=[pl.BlockSpec((tm, hq), lambda i: (i, 0)),
                   pl.BlockSpec((tm, hq), lambda i: (i, 0)),
                   pl.BlockSpec((tm, hv), lambda i: (i, 0))],
        out_shape=[jax.ShapeDtypeStruct((TOKENS, hq), BF16),
                   jax.ShapeDtypeStruct((TOKENS, hq), BF16),
                   jax.ShapeDtypeStruct((TOKENS, hv), BF16)],
        compiler_params=_cparams(("parallel",), V7X_VMEM_LIMIT),
        name="mla_up",
    )(proj, q_norm_g.reshape(1, -1), kv_norm_g.reshape(1, -1), wq_p, wkv_p, cos, sin)


def _masked_scores(s, diag):
    tq, tk = s.shape
    qc = lax.broadcasted_iota(jnp.int32, (tq, tk), 0) // CHUNK
    kc = (lax.broadcasted_iota(jnp.int32, (tq, tk), 1) + diag * tk) // CHUNK
    return jnp.where(kc <= qc, s, NEG_INF)


def _key_rows(j):
    return pl.ds(pl.multiple_of(j * TK, TK), TK)


def _softmax_pv(s, vaug, m_s, acc_s):
    tq, tk = s.shape
    m_prev = m_s[...]
    m_new = jnp.maximum(m_prev, jnp.max(s, axis=1, keepdims=True))
    alpha = jnp.exp(m_prev - m_new)
    p = jnp.exp(s - jnp.tile(m_new, (1, tk // LANES)))
    pv = jnp.dot(p.astype(BF16), vaug, preferred_element_type=F32)
    acc_s[...] = jnp.tile(alpha, (1, 2)) * acc_s[...] + pv
    m_s[...] = m_new


def _attn_sweep(qi, scores, vaug_s, states):
    n_diag = TQ // TK
    n_full = qi * n_diag

    def consume(ss, j):
        vaug = vaug_s[_key_rows(j), :]
        for s, (m_s, acc_s) in zip(ss, states):
            _softmax_pv(s, vaug, m_s, acc_s)

    pending, idx = scores(n_full, 0), n_full
    for d in range(1, n_diag):
        nxt = scores(n_full + d, d)
        consume(pending, idx)
        pending, idx = nxt, n_full + d

    def body(j, carry):
        nxt = scores(j, None)
        consume(*carry)
        return nxt, j

    consume(*lax.fori_loop(0, n_full, body, (pending, idx)))


def _init_attn_state(v_ref, vaug_s, states):
    @pl.when(pl.program_id(2) == 0)
    def _():
        vaug_s[:, :LANES] = v_ref[...]
        vaug_s[:, LANES:] = jnp.ones((vaug_s.shape[0], LANES), vaug_s.dtype)

    for m_s, acc_s in states:
        m_s[...] = jnp.full(m_s.shape, -jnp.inf, F32)
        acc_s[...] = jnp.zeros(acc_s.shape, F32)


def _attn_scratch(n_states):
    return ([pltpu.VMEM((SEQ, 2 * LANES), BF16)]
            + [pltpu.VMEM((TQ, LANES), F32), pltpu.VMEM((TQ, 2 * LANES), F32)] * n_states)


def _normalised(acc_s):
    return acc_s[:, :LANES] * (1.0 / acc_s[:, LANES:])


def _mla_attn_kernel(q_ref, k_ref, v_ref, o_ref, vaug_s, m_s, acc_s):
    tq = q_ref.shape[0]
    qi = pl.program_id(2)
    states = ((m_s, acc_s),)
    _init_attn_state(v_ref, vaug_s, states)
    q = q_ref[...]

    def scores(j, diag):
        s = _nt_dot(q, k_ref[_key_rows(j), :])
        return (s if diag is None else _masked_scores(s, diag),)

    _attn_sweep(qi, scores, vaug_s, states)
    o_ref[...] = _normalised(acc_s).astype(o_ref.dtype)


def _mla_attn(qa, ka, va):
    tq = TQ
    nq = SEQ // tq
    return pl.pallas_call(
        _mla_attn_kernel,
        grid=(BATCH, MLA_HEADS, nq),
        in_specs=[pl.BlockSpec((tq, MLA_QK_PAD), lambda b, h, i: (b * nq + i, h)),
                  pl.BlockSpec((SEQ, MLA_QK_PAD), lambda b, h, i: (b, h)),
                  pl.BlockSpec((SEQ, MLA_V), lambda b, h, i: (b, h))],
        out_specs=pl.BlockSpec((tq, MLA_V), lambda b, h, i: (b * nq + i, h)),
        out_shape=jax.ShapeDtypeStruct((TOKENS, MLA_HEADS * MLA_V), BF16),
        scratch_shapes=_attn_scratch(1),
        compiler_params=_cparams(("parallel", "parallel", "arbitrary"), V7X_VMEM_LIMIT),
        name="mla_attn",
    )(qa, ka, va)


def _diff_attn_kernel(q_ref, k_ref, v_ref, lam_ref, g_ref, o_ref, vaug_s, m1, a1, m2, a2, *, lam_init):
    tq = q_ref.shape[0]
    qi = pl.program_id(2)
    states = ((m1, a1), (m2, a2))
    _init_attn_state(v_ref, vaug_s, states)
    q = q_ref[...]
    lane = lax.broadcasted_iota(jnp.int32, q.shape, 1)
    zero = jnp.zeros_like(q)
    q1 = jnp.where(lane < DIFF_HEAD_DIM, q, zero)
    q2 = jnp.where(lane >= DIFF_HEAD_DIM, q, zero)

    def scores(j, diag):
        k = k_ref[_key_rows(j), :]
        s1 = _nt_dot(q1, k)
        s2 = _nt_dot(q2, k)
        if diag is not None:
            s1, s2 = _masked_scores(s1, diag), _masked_scores(s2, diag)
        return (s1, s2)

    _attn_sweep(qi, scores, vaug_s, states)
    lp = lam_ref[...]
    lam = (jnp.exp(jnp.sum(lp[0:1] * lp[1:2], axis=1, keepdims=True))
           - jnp.exp(jnp.sum(lp[2:3] * lp[3:4], axis=1, keepdims=True)) + lam_init)
    od = _normalised(a1) - lam * _normalised(a2)
    o_ref[...] = (_rms(od) * g_ref[...] * (1.0 - lam_init)).astype(o_ref.dtype)


def _diff_attn(proj, diff_lambda, subln_g, layer):
    tq = TQ
    nq = SEQ // tq
    hd = 2 * DIFF_HEAD_DIM
    q0, k0, v0 = 1024 // hd, 2048 // hd, 3072 // hd
    lam_init = 0.8 - 0.6 * math.exp(-0.3 * layer)
    return pl.pallas_call(
        functools.partial(_diff_attn_kernel, lam_init=lam_init),
        grid=(BATCH, DIFF_HEADS, nq),
        in_specs=[pl.BlockSpec((tq, hd), lambda b, h, i: (b * nq + i, q0 + h)),
                  pl.BlockSpec((SEQ, hd), lambda b, h, i: (b, k0 + h)),
                  pl.BlockSpec((SEQ, hd), lambda b, h, i: (b, v0 + h)),
                  pl.BlockSpec((4, DIFF_HEAD_DIM), lambda b, h, i: (0, 0)),
                  pl.BlockSpec((1, hd), lambda b, h, i: (0, 0))],
        out_specs=pl.BlockSpec((tq, hd), lambda b, h, i: (b * nq + i, h)),
        out_shape=jax.ShapeDtypeStruct((TOKENS, DIFF_HEADS * hd), BF16),
        scratch_shapes=_attn_scratch(2),
        compiler_params=_cparams(("parallel", "parallel", "arbitrary"), V7X_VMEM_LIMIT),
        name="diff_attn",
    )(proj, proj, proj, diff_lambda, subln_g.reshape(1, hd))


def _post_residual(x, y, gate, g):
    return x + gate * (_rms(y) * g)


def _merge_out_kernel(ya_ref, yb_ref, ga_ref, gb_ref, bg_ref, wa_ref, wb_ref, wo_ref, x_ref, gate_ref, g_ref,
                      o_ref):
    a = jnp.dot(ya_ref[...], wa_ref[...], preferred_element_type=F32)
    b = jnp.dot(yb_ref[...], wb_ref[...], preferred_element_type=F32)
    bg = bg_ref[...]
    merged = (jax.nn.sigmoid(ga_ref[...].astype(F32) + bg[0:1]) * a
              + jax.nn.sigmoid(gb_ref[...].astype(F32) + bg[1:2]) * b)
    y = jnp.dot(merged.astype(BF16), wo_ref[...], preferred_element_type=F32)
    o_ref[...] = _post_residual(x_ref[...], y, gate_ref[0], g_ref[...])


def _resident(shape):
    return pl.BlockSpec(shape, lambda i: (0,) * len(shape), pipeline_mode=pl.Buffered(1))


def _merge_out(ya, yb, proj, b_gate, wa, wb, wo, x, gate, g):
    tm = TM_MERGE
    per_b = SEQ // tm
    ga_blk, gb_blk = 4096 // D_MODEL, 6144 // D_MODEL
    return pl.pallas_call(
        _merge_out_kernel,
        grid=(TOKENS // tm,),
        in_specs=[pl.BlockSpec((tm, 1024), lambda i: (i, 0)),
                  pl.BlockSpec((tm, 1024), lambda i: (i, 0)),
                  pl.BlockSpec((tm, D_MODEL), lambda i: (i, ga_blk)),
                  pl.BlockSpec((tm, D_MODEL), lambda i: (i, gb_blk)),
                  _resident((2, D_MODEL)),
                  _resident((1024, D_MODEL)),
                  _resident((1024, D_MODEL)),
                  _resident((D_MODEL, D_MODEL)),
                  pl.BlockSpec((tm, D_MODEL), lambda i: (i, 0)),
                  pl.BlockSpec((1, 1, D_MODEL), lambda i: (i // per_b, 0, 0)),
                  _resident((1, D_MODEL))],
        out_specs=pl.BlockSpec((tm, D_MODEL), lambda i: (i, 0)),
        out_shape=jax.ShapeDtypeStruct((TOKENS, D_MODEL), F32),
        compiler_params=_cparams(("parallel",), V7X_VMEM_LIMIT),
        name="merge_out",
    )(ya, yb, proj, proj, b_gate, wa, wb, wo, x, gate, g.reshape(1, D_MODEL))


def _new_weights(te_ref, m):
    return jnp.logical_or(m == 0, te_ref[m] != te_ref[jnp.maximum(m - 1, 0)])


def _gu_kernel(te_ref, nu_ref, x_ref, wg_ref, wu_ref, o_ref, wg_s, wu_s):
    m = pl.program_id(1)

    @pl.when(m < nu_ref[0])
    def _():
        @pl.when(_new_weights(te_ref, m))
        def _():
            wg_s[...] = wg_ref[0].astype(BF16)
            wu_s[...] = wu_ref[0].astype(BF16)

        x = x_ref[...]
        g = jnp.dot(x, wg_s[...], preferred_element_type=F32)
        u = jnp.dot(x, wu_s[...], preferred_element_type=F32)
        o_ref[...] = (g * jax.nn.sigmoid(g) * u).astype(o_ref.dtype)

    @pl.when(m >= nu_ref[0])
    def _():
        o_ref[...] = jnp.zeros(o_ref.shape, o_ref.dtype)


def _down_kernel(te_ref, nu_ref, x_ref, w_ref, o_ref, w_s):
    m = pl.program_id(1)

    @pl.when(m < nu_ref[0])
    def _():
        @pl.when(_new_weights(te_ref, m))
        def _():
            w_s[...] = w_ref[0].astype(BF16)

        o_ref[...] = jnp.dot(x_ref[...], w_s[...], preferred_element_type=F32)

    @pl.when(m >= nu_ref[0])
    def _():
        o_ref[...] = jnp.zeros(o_ref.shape, o_ref.dtype)


def _grouped_matmul(body, x, w, te, nu, tm, tn, n_w, out_dtype, name):
    rows, k = x.shape
    n_tiles = w.shape[-1] // (n_w * tn)
    clamp = lambda m, nu_ref: jnp.minimum(m, nu_ref[0] - 1)
    w_specs = [pl.BlockSpec((1, k, tn), lambda n, m, te_r, nu_r, off=i * n_tiles: (te_r[clamp(m, nu_r)], 0, n + off))
               for i in range(n_w)]
    return pl.pallas_call(
        body,
        grid_spec=pltpu.PrefetchScalarGridSpec(
            num_scalar_prefetch=2,
            grid=(n_tiles, rows // tm),
            in_specs=[pl.BlockSpec((tm, k), lambda n, m, te_r, nu_r: (clamp(m, nu_r), 0))] + w_specs,
            out_specs=pl.BlockSpec((tm, tn), lambda n, m, te_r, nu_r: (m, n)),
            scratch_shapes=[pltpu.VMEM((k, tn), BF16)] * n_w),
        out_shape=jax.ShapeDtypeStruct((rows, n_tiles * tn), out_dtype),
        compiler_params=_cparams(("arbitrary", "arbitrary"), V7X_VMEM_LIMIT_GROUPED),
        name=name,
    )(te, nu, x, *([w] * n_w))


def _grouped_gu(x, w_gu, te, nu, tm, tn):
    return _grouped_matmul(_gu_kernel, x, w_gu, te, nu, tm, tn, 2, BF16, "grouped_gate_up")


def _grouped_down(x, w_down, te, nu, tm, tn):
    return _grouped_matmul(_down_kernel, x, w_down, te, nu, tm, tn, 1, F32, "grouped_down")


def _post_kernel(y_ref, x_ref, gate_ref, g_ref, o_ref):
    o_ref[...] = _post_residual(x_ref[...], y_ref[...], gate_ref[0], g_ref[...])


def _post(y, x, gate, g):
    tm = TM_NORM
    per_b = SEQ // tm
    return pl.pallas_call(
        _post_kernel,
        grid=(TOKENS // tm,),
        in_specs=[pl.BlockSpec((tm, D_MODEL), lambda i: (i, 0)),
                  pl.BlockSpec((tm, D_MODEL), lambda i: (i, 0)),
                  pl.BlockSpec((1, 1, D_MODEL), lambda i: (i // per_b, 0, 0)),
                  pl.BlockSpec((1, D_MODEL), lambda i: (0, 0))],
        out_specs=pl.BlockSpec((tm, D_MODEL), lambda i: (i, 0)),
        out_shape=jax.ShapeDtypeStruct((TOKENS, D_MODEL), F32),
        compiler_params=_cparams(("parallel",), V7X_VMEM_LIMIT),
        name="post_norm_residual",
    )(y, x, gate, g.reshape(1, D_MODEL))


def _row_copy(src_hbm, row, dst, slot, sem):
    return pltpu.make_async_copy(src_hbm.at[pl.ds(row, 1)], dst.at[pl.ds(slot, 1)], sem)


def _dispatch_kernel(src_ref, nu_ref, h_hbm, o_ref, buf, sem):
    i = pl.program_id(0)
    tm = buf.shape[0]

    @pl.when(i < nu_ref[0])
    def _():
        base = i * tm

        def issue(pair, c):
            for q in range(N_DMA_QUEUES):
                r = pair * N_DMA_QUEUES + q
                _row_copy(h_hbm, src_ref[base + r], buf, r, sem).start(priority=q)
            return c

        def wait(r, c):
            _row_copy(h_hbm, 0, buf, r, sem).wait()
            return c

        lax.fori_loop(0, tm // N_DMA_QUEUES, issue, 0)
        lax.fori_loop(0, tm, wait, 0)
        o_ref[...] = buf[...].astype(o_ref.dtype)

    @pl.when(i >= nu_ref[0])
    def _():
        o_ref[...] = jnp.zeros(o_ref.shape, o_ref.dtype)


def _dispatch(h, src, nu):
    tm = TM_DISPATCH
    return pl.pallas_call(
        _dispatch_kernel,
        grid_spec=pltpu.PrefetchScalarGridSpec(
            num_scalar_prefetch=2,
            grid=(N_SORT_PAD // tm,),
            in_specs=[pl.BlockSpec(memory_space=pl.ANY)],
            out_specs=pl.BlockSpec((tm, D_MODEL), lambda i, s_r, nu_r: (i, 0)),
            scratch_shapes=[pltpu.VMEM((tm, D_MODEL), F32), pltpu.SemaphoreType.DMA(())]),
        out_shape=jax.ShapeDtypeStruct((N_SORT_PAD, D_MODEL), BF16),
        compiler_params=_cparams(("arbitrary",)),
        name="moe_dispatch",
    )(src, nu, h)


def _combine_kernel(p0_ref, p1_ref, ys_hbm, wt_ref, x_ref, gate_ref, g_ref, o_ref, buf0, buf1, sem):
    i = pl.program_id(0)
    tm = buf0.shape[0]
    base = i * tm

    def issue(r, c):
        _row_copy(ys_hbm, p0_ref[base + r], buf0, r, sem).start(priority=0)
        _row_copy(ys_hbm, p1_ref[base + r], buf1, r, sem).start(priority=1)
        return c

    def wait(r, c):
        _row_copy(ys_hbm, 0, buf0, r, sem).wait()
        _row_copy(ys_hbm, 0, buf1, r, sem).wait()
        return c

    lax.fori_loop(0, tm, issue, 0)
    lax.fori_loop(0, tm, wait, 0)
    wt = wt_ref[...]
    y = wt[:, 0:1] * buf0[...] + wt[:, 1:2] * buf1[...]
    o_ref[...] = _post_residual(x_ref[...], y, gate_ref[0], g_ref[...])


def _combine(ys, pos0, pos1, wt, x, gate, g):
    tm = TM_COMBINE
    per_b = SEQ // tm
    return pl.pallas_call(
        _combine_kernel,
        grid_spec=pltpu.PrefetchScalarGridSpec(
            num_scalar_prefetch=2,
            grid=(TOKENS // tm,),
            in_specs=[pl.BlockSpec(memory_space=pl.ANY),
                      pl.BlockSpec((tm, TOP_K), lambda i, a, b: (i, 0)),
                      pl.BlockSpec((tm, D_MODEL), lambda i, a, b: (i, 0)),
                      pl.BlockSpec((1, 1, D_MODEL), lambda i, a, b: (i // per_b, 0, 0)),
                      pl.BlockSpec((1, D_MODEL), lambda i, a, b: (0, 0))],
            out_specs=pl.BlockSpec((tm, D_MODEL), lambda i, a, b: (i, 0)),
            scratch_shapes=[pltpu.VMEM((tm, D_MODEL), F32), pltpu.VMEM((tm, D_MODEL), F32),
                            pltpu.SemaphoreType.DMA(())]),
        out_shape=jax.ShapeDtypeStruct((TOKENS, D_MODEL), F32),
        compiler_params=_cparams(("arbitrary",)),
        name="moe_combine",
    )(pos0, pos1, ys, wt, x, gate, g.reshape(1, D_MODEL))


def _routing_tables(top_i, rank, counts):
    tm = TM_MOE
    n_tiles = N_SORT_PAD // tm
    experts = jnp.arange(N_EXPERTS, dtype=jnp.int32)
    padded = ((counts + tm - 1) // tm) * tm
    ends = jnp.cumsum(padded)
    offs = ends - padded
    group_start = jnp.sum((top_i[:, :, None] == experts) * offs, axis=-1)
    pos = (group_start + rank).astype(jnp.int32)
    tok = jnp.tile(jnp.arange(TOKENS, dtype=jnp.int32), TOP_K)
    src = jnp.zeros((N_SORT_PAD,), jnp.int32).at[pos.reshape(-1)].set(tok)
    tile_start = jnp.arange(n_tiles, dtype=jnp.int32) * tm
    te = jnp.sum((tile_start[:, None] >= ends[None, :]).astype(jnp.int32), axis=1)
    te = jnp.minimum(te, N_EXPERTS - 1)
    nu = (ends[-1] // tm).astype(jnp.int32).reshape(1)
    return src, te, nu, pos[0], pos[1]


def _split_w_in(w):
    lat = MLA_Q_RANK + MLA_KV_RANK + MLA_ROPE
    w_lat = jnp.zeros((D_MODEL, TN_IN), BF16).at[:, :lat].set(w[:, :lat].astype(BF16))
    return w_lat, w[:, lat:].astype(BF16)


def _pad_w_q_up(w):
    w = w.reshape(MLA_Q_RANK, MLA_HEADS, MLA_NOPE + MLA_ROPE)
    pad = jnp.zeros((MLA_Q_RANK, MLA_HEADS, MLA_QK_PAD - MLA_NOPE - MLA_ROPE), w.dtype)
    return jnp.concatenate([w, pad], axis=2).reshape(MLA_Q_RANK, MLA_HEADS * MLA_QK_PAD).astype(BF16)


def _perm_w_kv_up(w):
    w = w.reshape(MLA_KV_RANK, MLA_HEADS, MLA_NOPE + MLA_V)
    k = w[:, :, :MLA_NOPE].reshape(MLA_KV_RANK, MLA_HEADS * MLA_NOPE)
    v = w[:, :, MLA_NOPE:].reshape(MLA_KV_RANK, MLA_HEADS * MLA_V)
    return jnp.concatenate([k, v], axis=1).astype(BF16)


def _one_group(tm):
    n_tiles = TOKENS // tm
    return jnp.zeros((n_tiles,), jnp.int32), jnp.full((1,), n_tiles, jnp.int32)


def kernel(x, c, positions, ada_w, ada_b, norm_g, w_in, q_norm_g, kv_norm_g, w_q_up, w_kv_up, diff_lambda,
           diff_subln_g, w_branch_a, w_branch_b, b_gate, w_out, dense_w_gu, dense_w_down, router_w, moe_w_gu,
           moe_w_down):
    assert x.shape == (BATCH, SEQ, D_MODEL) and c.shape == (BATCH, D_MODEL)
    assert w_in.shape == (DEPTH, D_MODEL, 8000)
    xt = x.reshape(TOKENS, D_MODEL)
    cos, sin = _rope_tables(positions)
    mod = _adaln(c, ada_w, ada_b)

    for l in range(DEPTH):
        shift, scale, gate = (mod[l, 0, :, :, i * D_MODEL:(i + 1) * D_MODEL] for i in range(3))
        h = _norm_mod(xt, norm_g[l, 0], scale, shift)
        proj = _in_proj(h, *_split_w_in(w_in[l]), cos, sin)
        qa, ka, va = _mla_up(proj, q_norm_g[l], kv_norm_g[l], _pad_w_q_up(w_q_up[l]), _perm_w_kv_up(w_kv_up[l]),
                             cos, sin)
        ya = _mla_attn(qa, ka, va)
        yb = _diff_attn(proj, diff_lambda[l], diff_subln_g[l], l)
        xt = _merge_out(ya, yb, proj, b_gate[l], w_branch_a[l].astype(BF16), w_branch_b[l].astype(BF16),
                        w_out[l].astype(BF16), xt, gate, norm_g[l, 1])

        shift, scale, gate = (mod[l, 1, :, :, i * D_MODEL:(i + 1) * D_MODEL] for i in range(3))
        if l % 2 == 0:
            h = _norm_mod(xt, norm_g[l, 2], scale, shift)
            act = _grouped_gu(h, dense_w_gu[l // 2][None], *_one_group(TM_DENSE_GU), TM_DENSE_GU, TN_DENSE_GU)
            y = _grouped_down(act, dense_w_down[l // 2][None], *_one_group(TM_DENSE_DOWN), TM_DENSE_DOWN,
                              TN_DENSE_DOWN)
            xt = _post(y, xt, gate, norm_g[l, 3])
        else:
            h, top_i, top_w, rank, cnt = _norm_mod_route(xt, norm_g[l, 2], scale, shift, router_w[l // 2])
            src, te, nu, pos0, pos1 = _routing_tables(top_i, rank, cnt[:, 0].astype(jnp.int32))
            xs = _dispatch(h, src, nu * (TM_MOE // TM_DISPATCH))
            act = _grouped_gu(xs, moe_w_gu[l // 2], te, nu, TM_MOE, TN_MOE_GU)
            ys = _grouped_down(act, moe_w_down[l // 2], te, nu, TM_MOE, TN_MOE_DOWN)
            xt = _combine(ys, pos0, pos1, top_w.T, xt, gate, norm_g[l, 3])
    return xt.reshape(BATCH, SEQ, D_MODEL)
```

```python
import functools
import math

import jax
import jax.numpy as jnp
import numpy as np
from jax import lax
from jax.experimental import pallas as pl
from jax.experimental.pallas import tpu as pltpu

F32 = jnp.float32
BF16 = jnp.bfloat16

D_MODEL = 2048
BATCH = 2
SEQ = 4096
TOKENS = BATCH * SEQ
DEPTH = 2
CHUNK = 64
ROPE_THETA = 10000.0
EPS = 1e-6
NEG_INF = -1e30

MLA_HEADS = 8
MLA_Q_RANK = 512
MLA_KV_RANK = 256
MLA_NOPE = 128
MLA_ROPE = 64
MLA_V = 128
MLA_QK_PAD = 256

DIFF_HEADS = 8
DIFF_HEAD_DIM = 64

D_FF_DENSE = 5632
N_EXPERTS = 8
TOP_K = 2
D_FF_EXPERT = 7168

D_IN_PAD = 8192

LANES = 128
N_DMA_QUEUES = 2
V7X_VMEM_LIMIT = 56 * 1024 * 1024
V7X_VMEM_LIMIT_GROUPED = 60 * 1024 * 1024

TM_NORM = 512
TM_IN = 2048
TN_IN = 1024
TM_UP = 512
TQ = 1024
TK = 512
TM_MERGE = 256
TM_DENSE_GU = 1024
TM_DENSE_DOWN = 512
TN_DENSE_GU = 512
TN_DENSE_DOWN = 512
TM_MOE = 512
TM_DISPATCH = 256
TN_MOE_GU = 1024
TN_MOE_DOWN = 512
N_SORT_PAD = TOP_K * TOKENS + N_EXPERTS * TM_MOE
TM_COMBINE = 256


def _cparams(sem, vmem=None):
    return pltpu.CompilerParams(dimension_semantics=sem, vmem_limit_bytes=vmem)


def _nt_dot(a, b):
    return lax.dot_general(a, b, (((1,), (1,)), ((), ())), preferred_element_type=F32)


def _rms(y):
    return y * lax.rsqrt(jnp.mean(y * y, axis=-1, keepdims=True) + EPS)


def _rope_lanes(blk, cos, sin_signed, first_half):
    rot = jnp.where(first_half, pltpu.roll(blk, 96, 1), pltpu.roll(blk, 32, 1))
    return blk * cos + rot * sin_signed


def _first_half_mask(rows):
    lane = lax.broadcasted_iota(jnp.int32, (rows, LANES), 1)
    return (lane % 64) < 32


def _rope_tables_kernel(pos_ref, freq_ref, sign_ref, cos_ref, sin_ref):
    ang = pos_ref[...].astype(F32) * freq_ref[...]
    cos_ref[...] = jnp.cos(ang)
    sin_ref[...] = jnp.sin(ang) * sign_ref[...]


def _rope_tables(positions):
    half = MLA_ROPE // 2
    inv_freq = ROPE_THETA ** (-jnp.arange(0, MLA_ROPE, 2, dtype=F32) / MLA_ROPE)
    freq = jnp.tile(inv_freq, LANES // half).reshape(1, LANES)
    sign = jnp.tile(jnp.concatenate([-jnp.ones((half,), F32), jnp.ones((half,), F32)]), LANES // MLA_ROPE)
    sign = sign.reshape(1, LANES)
    pos = positions.reshape(TOKENS, 1)
    tm = 1024
    return pl.pallas_call(
        _rope_tables_kernel,
        grid=(TOKENS // tm,),
        in_specs=[pl.BlockSpec((tm, 1), lambda i: (i, 0)),
                  pl.BlockSpec((1, LANES), lambda i: (0, 0)),
                  pl.BlockSpec((1, LANES), lambda i: (0, 0))],
        out_specs=[pl.BlockSpec((tm, LANES), lambda i: (i, 0)),
                   pl.BlockSpec((tm, LANES), lambda i: (i, 0))],
        out_shape=[jax.ShapeDtypeStruct((TOKENS, LANES), F32)] * 2,
        compiler_params=_cparams(("parallel",)),
        name="rope_tables",
    )(pos, freq, sign)


def _adaln_kernel(c_ref, w_ref, b_ref, o_ref):
    @pl.when(pl.program_id(1) == 0)
    def _():
        o_ref[0] = jnp.broadcast_to(b_ref[0], o_ref.shape[1:])

    c = c_ref[...]
    cs = (c * jax.nn.sigmoid(c)).astype(BF16)
    o_ref[0] += jnp.dot(cs, w_ref[0].astype(BF16), preferred_element_type=F32)


def _adaln(c, ada_w, ada_b):
    n_mod = DEPTH * 2
    w = ada_w.reshape(n_mod, D_MODEL, 3 * D_MODEL)
    b = ada_b.reshape(n_mod, 1, 3 * D_MODEL)
    c8 = jnp.zeros((8, D_MODEL), F32).at[:BATCH].set(c)
    tk = 256
    out = pl.pallas_call(
        _adaln_kernel,
        grid=(n_mod, D_MODEL // tk),
        in_specs=[pl.BlockSpec((8, tk), lambda m, k: (0, k)),
                  pl.BlockSpec((1, tk, 3 * D_MODEL), lambda m, k: (m, k, 0)),
                  pl.BlockSpec((1, 1, 3 * D_MODEL), lambda m, k: (m, 0, 0))],
        out_specs=pl.BlockSpec((1, 8, 3 * D_MODEL), lambda m, k: (m, 0, 0)),
        out_shape=jax.ShapeDtypeStruct((n_mod, 8, 3 * D_MODEL), F32),
        compiler_params=_cparams(("parallel", "arbitrary"), V7X_VMEM_LIMIT),
        name="adaln",
    )(c8, w, b)
    mod = out[:, :BATCH, :].reshape(DEPTH, 2, BATCH, 1, 3 * D_MODEL)
    return mod


def _norm_mod_kernel(x_ref, g_ref, sc_ref, sh_ref, h_ref):
    h = _rms(x_ref[...]) * g_ref[...] * (1.0 + sc_ref[0]) + sh_ref[0]
    h_ref[...] = h.astype(h_ref.dtype)


def _norm_mod_route_kernel(x_ref, g_ref, sc_ref, sh_ref, rwh_ref, rwl_ref, h_ref, idx_ref, wt_ref, rank_ref,
                           cnt_ref, cnt_s):
    @pl.when(pl.program_id(0) == 0)
    def _():
        cnt_s[...] = jnp.zeros(cnt_s.shape, F32)

    h = _rms(x_ref[...]) * g_ref[...] * (1.0 + sc_ref[0]) + sh_ref[0]
    h_ref[...] = h
    h_hi = h.astype(BF16)
    h_lo = (h - h_hi.astype(F32)).astype(BF16)
    logits = (jnp.dot(h_hi, rwh_ref[...], preferred_element_type=F32)
              + jnp.dot(h_hi, rwl_ref[...], preferred_element_type=F32)
              + jnp.dot(h_lo, rwh_ref[...], preferred_element_type=F32))
    lt = logits.T[:N_EXPERTS, :]
    eid = lax.broadcasted_iota(jnp.int32, lt.shape, 0)
    m1 = jnp.max(lt, axis=0, keepdims=True)
    i1 = jnp.min(jnp.where(lt == m1, eid, N_EXPERTS), axis=0, keepdims=True)
    lt2 = jnp.where(eid == i1, -jnp.inf, lt)
    m2 = jnp.max(lt2, axis=0, keepdims=True)
    i2 = jnp.min(jnp.where(lt2 == m2, eid, N_EXPERTS), axis=0, keepdims=True)
    e2 = jnp.exp(m2 - m1)
    w1 = 1.0 / (1.0 + e2)
    idx_ref[0:1, :] = i1
    idx_ref[1:2, :] = i2
    wt_ref[0:1, :] = w1
    wt_ref[1:2, :] = e2 * w1
    tm = lt.shape[1]
    chosen = jnp.logical_or(eid == i1, eid == i2)
    upper = (lax.broadcasted_iota(jnp.int32, (tm, tm), 0) < lax.broadcasted_iota(jnp.int32, (tm, tm), 1))
    before = jnp.dot(chosen.astype(BF16), upper.astype(BF16), preferred_element_type=F32) + cnt_s[:, 0:1]
    rank_ref[0:1, :] = jnp.sum(jnp.where(eid == i1, before, 0.0), axis=0, keepdims=True).astype(jnp.int32)
    rank_ref[1:2, :] = jnp.sum(jnp.where(eid == i2, before, 0.0), axis=0, keepdims=True).astype(jnp.int32)
    cnt_s[...] = cnt_s[...] + jnp.sum(chosen.astype(F32), axis=1, keepdims=True)
    cnt_ref[...] = cnt_s[...]


def _mod_specs(tm):
    per_b = SEQ // tm
    return [pl.BlockSpec((tm, D_MODEL), lambda i: (i, 0)),
            pl.BlockSpec((1, D_MODEL), lambda i: (0, 0)),
            pl.BlockSpec((1, 1, D_MODEL), lambda i: (i // per_b, 0, 0)),
            pl.BlockSpec((1, 1, D_MODEL), lambda i: (i // per_b, 0, 0))]


def _norm_mod(x, g, scale, shift):
    tm = TM_NORM
    return pl.pallas_call(
        _norm_mod_kernel,
        grid=(TOKENS // tm,),
        in_specs=_mod_specs(tm),
        out_specs=pl.BlockSpec((tm, D_MODEL), lambda i: (i, 0)),
        out_shape=jax.ShapeDtypeStruct((TOKENS, D_MODEL), BF16),
        compiler_params=_cparams(("parallel",)),
        name="norm_mod",
    )(x, g.reshape(1, D_MODEL), scale, shift)


def _norm_mod_route(x, g, scale, shift, router_w):
    tm = TM_NORM
    rw = jnp.zeros((D_MODEL, LANES), F32).at[:, :N_EXPERTS].set(router_w)
    rw_hi = rw.astype(BF16)
    rw_lo = (rw - rw_hi.astype(F32)).astype(BF16)
    return pl.pallas_call(
        _norm_mod_route_kernel,
        grid=(TOKENS // tm,),
        in_specs=_mod_specs(tm) + [pl.BlockSpec((D_MODEL, LANES), lambda i: (0, 0)),
                                   pl.BlockSpec((D_MODEL, LANES), lambda i: (0, 0))],
        out_specs=[pl.BlockSpec((tm, D_MODEL), lambda i: (i, 0)),
                   pl.BlockSpec((TOP_K, tm), lambda i: (0, i)),
                   pl.BlockSpec((TOP_K, tm), lambda i: (0, i)),
                   pl.BlockSpec((TOP_K, tm), lambda i: (0, i)),
                   pl.BlockSpec((N_EXPERTS, LANES), lambda i: (0, 0))],
        out_shape=[jax.ShapeDtypeStruct((TOKENS, D_MODEL), F32),
                   jax.ShapeDtypeStruct((TOP_K, TOKENS), jnp.int32),
                   jax.ShapeDtypeStruct((TOP_K, TOKENS), F32),
                   jax.ShapeDtypeStruct((TOP_K, TOKENS), jnp.int32),
                   jax.ShapeDtypeStruct((N_EXPERTS, LANES), F32)],
        scratch_shapes=[pltpu.VMEM((N_EXPERTS, LANES), F32)],
        compiler_params=_cparams(("arbitrary",)),
        name="norm_mod_route",
    )(x, g.reshape(1, D_MODEL), scale, shift, rw_hi, rw_lo)


def _in_proj_kernel(h_ref, wlat_ref, wrest_ref, cos_ref, sin_ref, o_ref):
    n = pl.program_id(0)
    tm, tn = o_ref.shape

    def project(w_ref):
        return jnp.dot(h_ref[...], w_ref[...], preferred_element_type=F32)

    def rope_cols(acc, lo, hi, mult):
        fh = _first_half_mask(tm)
        cos, sin = cos_ref[...], sin_ref[...]
        for c in range(lo // LANES, hi // LANES):
            blk = acc[:, c * LANES:(c + 1) * LANES]
            o_ref[:, c * LANES:(c + 1) * LANES] = (_rope_lanes(blk, cos, sin, fh) * mult).astype(o_ref.dtype)

    @pl.when(n == 0)
    def _():
        acc = project(wlat_ref)
        lat = MLA_Q_RANK + MLA_KV_RANK
        o_ref[:, :lat] = acc[:, :lat].astype(o_ref.dtype)
        rope_cols(acc, lat, tn, 1.0)

    @pl.when(n == 1)
    def _():
        rope_cols(project(wrest_ref), 0, tn, DIFF_HEAD_DIM ** -0.5)

    @pl.when(n == 2)
    def _():
        rope_cols(project(wrest_ref), 0, tn, 1.0)

    @pl.when(n >= 3)
    def _():
        o_ref[...] = project(wrest_ref).astype(o_ref.dtype)


def _in_proj(h, w_lat, w_rest, cos, sin):
    tm, tn = TM_IN, TN_IN
    return pl.pallas_call(
        _in_proj_kernel,
        grid=(D_IN_PAD // tn, TOKENS // tm),
        in_specs=[pl.BlockSpec((tm, D_MODEL), lambda n, m: (m, 0)),
                  pl.BlockSpec((D_MODEL, tn), lambda n, m: (0, 0)),
                  pl.BlockSpec((D_MODEL, tn), lambda n, m: (0, jnp.maximum(n - 1, 0))),
                  pl.BlockSpec((tm, LANES), lambda n, m: (m, 0)),
                  pl.BlockSpec((tm, LANES), lambda n, m: (m, 0))],
        out_specs=pl.BlockSpec((tm, tn), lambda n, m: (m, n)),
        out_shape=jax.ShapeDtypeStruct((TOKENS, D_IN_PAD), BF16),
        compiler_params=_cparams(("parallel", "parallel"), V7X_VMEM_LIMIT),
        name="in_proj",
    )(h, w_lat, w_rest, cos, sin)


def _mla_up_kernel(lat_ref, qg_ref, kvg_ref, wq_ref, wkv_ref, cos_ref, sin_ref, q_ref, k_ref, v_ref):
    tm = lat_ref.shape[0]
    scale = (MLA_NOPE + MLA_ROPE) ** -0.5
    q_lat = lat_ref[:, :MLA_Q_RANK].astype(F32)
    kv_lat = lat_ref[:, MLA_Q_RANK:MLA_Q_RANK + MLA_KV_RANK].astype(F32)
    k_pe = lat_ref[:, MLA_Q_RANK + MLA_KV_RANK:MLA_Q_RANK + MLA_KV_RANK + LANES]
    qn = (_rms(q_lat) * qg_ref[...]).astype(BF16)
    kvn = (_rms(kv_lat) * kvg_ref[...]).astype(BF16)
    q = jnp.dot(qn, wq_ref[...], preferred_element_type=F32)
    kv = jnp.dot(kvn, wkv_ref[...], preferred_element_type=F32)
    fh = _first_half_mask(tm)
    cos, sin = cos_ref[...], sin_ref[...]
    for h in range(MLA_HEADS):
        c0 = h * MLA_QK_PAD
        q_ref[:, c0:c0 + LANES] = (q[:, c0:c0 + LANES] * scale).astype(BF16)
        q_pe = _rope_lanes(q[:, c0 + LANES:c0 + 2 * LANES], cos, sin, fh)
        q_ref[:, c0 + LANES:c0 + 2 * LANES] = (q_pe * scale).astype(BF16)
        k_ref[:, c0:c0 + LANES] = kv[:, h * LANES:(h + 1) * LANES].astype(BF16)
        k_ref[:, c0 + LANES:c0 + 2 * LANES] = k_pe
    v_ref[...] = kv[:, MLA_HEADS * MLA_NOPE:].astype(BF16)


def _mla_up(proj, q_norm_g, kv_norm_g, wq_p, wkv_p, cos, sin):
    tm = TM_UP
    lat_w = 1024
    hq = MLA_HEADS * MLA_QK_PAD
    hv = MLA_HEADS * MLA_V
    const = lambda i: (0, 0)
    return pl.pallas_call(
        _mla_up_kernel,
        grid=(TOKENS // tm,),
        in_specs=[pl.BlockSpec((tm, lat_w), lambda i: (i, 0)),
                  pl.BlockSpec((1, MLA_Q_RANK), const),
                  pl.BlockSpec((1, MLA_KV_RANK), const),
                  pl.BlockSpec((MLA_Q_RANK, hq), const),
                  pl.BlockSpec((MLA_KV_RANK, 2 * hv), const),
                  pl.BlockSpec((tm, LANES), lambda i: (i, 0)),
                  pl.BlockSpec((tm, LANES), lambda i: (i, 0))],
        out_specs=[pl.BlockSpec((tm, hq), lambda i: (i, 0)),
                   pl.BlockSpec((tm, hq), lambda i: (i, 0)),
                   pl.BlockSpec((tm, hv), lambda i: (i, 0))],
        out_shape=[jax.ShapeDtypeStruct((TOKENS, hq), BF16),
                   jax.ShapeDtypeStruct((TOKENS, hq), BF16),
                   jax.ShapeDtypeStruct((TOKENS, hv), BF16)],
        compiler_params=_cparams(("parallel",), V7X_VMEM_LIMIT),
        name="mla_up",
    )(proj, q_norm_g.reshape(1, -1), kv_norm_g.reshape(1, -1), wq_p, wkv_p, cos, sin)


def _diag_masked(s):
    qc = lax.broadcasted_iota(jnp.int32, s.shape, 0) // CHUNK
    kc = lax.broadcasted_iota(jnp.int32, s.shape, 1) // CHUNK
    return jnp.where(kc <= qc, s, NEG_INF)


def _key_rows(j):
    return pl.ds(pl.multiple_of(j * TK, TK), TK)


def _softmax_pv(s, vaug, m_s, acc_s, row0):
    tk = s.shape[1]
    m_prev = m_s[row0:, :]
    m_new = jnp.maximum(m_prev, jnp.max(s, axis=1, keepdims=True))
    alpha = jnp.exp(m_prev - m_new)
    p = jnp.exp(s - jnp.tile(m_new, (1, tk // LANES)))
    pv = jnp.dot(p.astype(BF16), vaug, preferred_element_type=F32)
    acc_s[row0:, :] = jnp.tile(alpha, (1, 2)) * acc_s[row0:, :] + pv
    m_s[row0:, :] = m_new


def _attn_sweep(qi, scores, vaug_s, states):
    n_diag = TQ // TK
    n_full = qi * n_diag

    def consume(ss, j, row0):
        vaug = vaug_s[_key_rows(j), :]
        for s, (m_s, acc_s) in zip(ss, states):
            _softmax_pv(s, vaug, m_s, acc_s, row0)

    d = n_diag - 1
    pending, idx, row0 = scores(n_full + d, d * TK), n_full + d, d * TK
    for d in reversed(range(n_diag - 1)):
        nxt = scores(n_full + d, d * TK)
        consume(pending, idx, row0)
        pending, idx, row0 = nxt, n_full + d, d * TK

    def body(j, carry):
        nxt = scores(j, None)
        consume(*carry, 0)
        return nxt, j

    consume(*lax.fori_loop(0, n_full, body, (pending, idx)), 0)


def _init_attn_state(v_ref, vaug_s, states):
    @pl.when(pl.program_id(2) == 0)
    def _():
        vaug_s[:, :LANES] = v_ref[...]
        vaug_s[:, LANES:] = jnp.ones((vaug_s.shape[0], LANES), vaug_s.dtype)

    for m_s, acc_s in states:
        m_s[...] = jnp.full(m_s.shape, -jnp.inf, F32)
        acc_s[...] = jnp.zeros(acc_s.shape, F32)


def _attn_scratch(n_states):
    return ([pltpu.VMEM((SEQ, 2 * LANES), BF16)]
            + [pltpu.VMEM((TQ, LANES), F32), pltpu.VMEM((TQ, 2 * LANES), F32)] * n_states)


def _normalised(acc_s):
    return acc_s[:, :LANES] * (1.0 / acc_s[:, LANES:])


def _mla_attn_kernel(q_ref, k_ref, v_ref, o_ref, vaug_s, m_s, acc_s):
    qi = pl.program_id(2)
    states = ((m_s, acc_s),)
    _init_attn_state(v_ref, vaug_s, states)

    def scores(j, row0):
        k = k_ref[_key_rows(j), :]
        if row0 is None:
            return (_nt_dot(q_ref[...], k),)
        return (_diag_masked(_nt_dot(q_ref[row0:, :], k)),)

    _attn_sweep(qi, scores, vaug_s, states)
    o_ref[...] = _normalised(acc_s).astype(o_ref.dtype)


def _mla_attn(qa, ka, va):
    tq = TQ
    nq = SEQ // tq
    return pl.pallas_call(
        _mla_attn_kernel,
        grid=(BATCH, MLA_HEADS, nq),
        in_specs=[pl.BlockSpec((tq, MLA_QK_PAD), lambda b, h, i: (b * nq + i, h)),
                  pl.BlockSpec((SEQ, MLA_QK_PAD), lambda b, h, i: (b, h)),
                  pl.BlockSpec((SEQ, MLA_V), lambda b, h, i: (b, h))],
        out_specs=pl.BlockSpec((tq, MLA_V), lambda b, h, i: (b * nq + i, h)),
        out_shape=jax.ShapeDtypeStruct((TOKENS, MLA_HEADS * MLA_V), BF16),
        scratch_shapes=_attn_scratch(1),
        compiler_params=_cparams(("parallel", "parallel", "arbitrary"), V7X_VMEM_LIMIT),
        name="mla_attn",
    )(qa, ka, va)


def _diff_attn_kernel(q_ref, k_ref, v_ref, lam_ref, g_ref, o_ref, vaug_s, m1, a1, m2, a2, *, lam_init):
    qi = pl.program_id(2)
    states = ((m1, a1), (m2, a2))
    _init_attn_state(v_ref, vaug_s, states)
    q = q_ref[...]
    lane = lax.broadcasted_iota(jnp.int32, q.shape, 1)
    zero = jnp.zeros_like(q)
    q1 = jnp.where(lane < DIFF_HEAD_DIM, q, zero)
    q2 = jnp.where(lane >= DIFF_HEAD_DIM, q, zero)

    def scores(j, row0):
        k = k_ref[_key_rows(j), :]
        if row0 is None:
            return (_nt_dot(q1, k), _nt_dot(q2, k))
        return (_diag_masked(_nt_dot(q1[row0:, :], k)), _diag_masked(_nt_dot(q2[row0:, :], k)))

    _attn_sweep(qi, scores, vaug_s, states)
    lp = lam_ref[...]
    lam = (jnp.exp(jnp.sum(lp[0:1] * lp[1:2], axis=1, keepdims=True))
           - jnp.exp(jnp.sum(lp[2:3] * lp[3:4], axis=1, keepdims=True)) + lam_init)
    od = _normalised(a1) - lam * _normalised(a2)
    o_ref[...] = (_rms(od) * g_ref[...] * (1.0 - lam_init)).astype(o_ref.dtype)


def _diff_attn(proj, diff_lambda, subln_g, layer):
    tq = TQ
    nq = SEQ // tq
    hd = 2 * DIFF_HEAD_DIM
    q0, k0, v0 = 1024 // hd, 2048 // hd, 3072 // hd
    lam_init = 0.8 - 0.6 * math.exp(-0.3 * layer)
    return pl.pallas_call(
        functools.partial(_diff_attn_kernel, lam_init=lam_init),
        grid=(BATCH, DIFF_HEADS, nq),
        in_specs=[pl.BlockSpec((tq, hd), lambda b, h, i: (b * nq + i, q0 + h)),
                  pl.BlockSpec((SEQ, hd), lambda b, h, i: (b, k0 + h)),
                  pl.BlockSpec((SEQ, hd), lambda b, h, i: (b, v0 + h)),
                  pl.BlockSpec((4, DIFF_HEAD_DIM), lambda b, h, i: (0, 0)),
                  pl.BlockSpec((1, hd), lambda b, h, i: (0, 0))],
        out_specs=pl.BlockSpec((tq, hd), lambda b, h, i: (b * nq + i, h)),
        out_shape=jax.ShapeDtypeStruct((TOKENS, DIFF_HEADS * hd), BF16),
        scratch_shapes=_attn_scratch(2),
        compiler_params=_cparams(("parallel", "parallel", "arbitrary"), V7X_VMEM_LIMIT),
        name="diff_attn",
    )(proj, proj, proj, diff_lambda, subln_g.reshape(1, hd))


def _post_residual(x, y, gate, g):
    return x + gate * (_rms(y) * g)


def _merge_out_kernel(ya_ref, yb_ref, ga_ref, gb_ref, bg_ref, wa_ref, wb_ref, wo_ref, x_ref, gate_ref, g_ref,
                      o_ref):
    a = jnp.dot(ya_ref[...], wa_ref[...], preferred_element_type=F32)
    b = jnp.dot(yb_ref[...], wb_ref[...], preferred_element_type=F32)
    bg = bg_ref[...]
    merged = (jax.nn.sigmoid(ga_ref[...].astype(F32) + bg[0:1]) * a
              + jax.nn.sigmoid(gb_ref[...].astype(F32) + bg[1:2]) * b)
    y = jnp.dot(merged.astype(BF16), wo_ref[...], preferred_element_type=F32)
    o_ref[...] = _post_residual(x_ref[...], y, gate_ref[0], g_ref[...])


def _resident(shape):
    return pl.BlockSpec(shape, lambda i: (0,) * len(shape), pipeline_mode=pl.Buffered(1))


def _merge_out(ya, yb, proj, b_gate, wa, wb, wo, x, gate, g):
    tm = TM_MERGE
    per_b = SEQ // tm
    ga_blk, gb_blk = 4096 // D_MODEL, 6144 // D_MODEL
    return pl.pallas_call(
        _merge_out_kernel,
        grid=(TOKENS // tm,),
        in_specs=[pl.BlockSpec((tm, 1024), lambda i: (i, 0)),
                  pl.BlockSpec((tm, 1024), lambda i: (i, 0)),
                  pl.BlockSpec((tm, D_MODEL), lambda i: (i, ga_blk)),
                  pl.BlockSpec((tm, D_MODEL), lambda i: (i, gb_blk)),
                  _resident((2, D_MODEL)),
                  _resident((1024, D_MODEL)),
                  _resident((1024, D_MODEL)),
                  _resident((D_MODEL, D_MODEL)),
                  pl.BlockSpec((tm, D_MODEL), lambda i: (i, 0)),
                  pl.BlockSpec((1, 1, D_MODEL), lambda i: (i // per_b, 0, 0)),
                  _resident((1, D_MODEL))],
        out_specs=pl.BlockSpec((tm, D_MODEL), lambda i: (i, 0)),
        out_shape=jax.ShapeDtypeStruct((TOKENS, D_MODEL), F32),
        compiler_params=_cparams(("parallel",), V7X_VMEM_LIMIT),
        name="merge_out",
    )(ya, yb, proj, proj, b_gate, wa, wb, wo, x, gate, g.reshape(1, D_MODEL))


def _new_weights(te_ref, m):
    return jnp.logical_or(m == 0, te_ref[m] != te_ref[jnp.maximum(m - 1, 0)])


def _grouped_kernel(te_ref, nv_ref, nu_ref, x_ref, *refs, tile_fn):
    n_w = (len(refs) - 1) // 2
    w_refs, o_ref, w_bf16 = refs[:n_w], refs[n_w], refs[n_w + 1:]
    m = pl.program_id(1)
    tm = x_ref.shape[0]
    in_use = m < nu_ref[0]

    @pl.when(jnp.logical_and(in_use, _new_weights(te_ref, m)))
    def _():
        for w_ref, w_s in zip(w_refs, w_bf16):
            w_s[...] = w_ref[0].astype(BF16)

    @pl.when(jnp.logical_and(in_use, nv_ref[m] > tm // 2))
    def _():
        o_ref[...] = tile_fn(x_ref[...], *w_bf16).astype(o_ref.dtype)

    @pl.when(jnp.logical_and(in_use, nv_ref[m] <= tm // 2))
    def _():
        o_ref[:tm // 2, :] = tile_fn(x_ref[:tm // 2, :], *w_bf16).astype(o_ref.dtype)
        o_ref[tm // 2:, :] = jnp.zeros((tm // 2, o_ref.shape[1]), o_ref.dtype)

    @pl.when(jnp.logical_not(in_use))
    def _():
        o_ref[...] = jnp.zeros(o_ref.shape, o_ref.dtype)


def _swiglu_tile(x, wg_s, wu_s):
    g = jnp.dot(x, wg_s[...], preferred_element_type=F32)
    u = jnp.dot(x, wu_s[...], preferred_element_type=F32)
    return g * jax.nn.sigmoid(g) * u


def _down_tile(x, w_s):
    return jnp.dot(x, w_s[...], preferred_element_type=F32)


def _grouped_matmul(tile_fn, x, w, tables, tm, tn, n_w, out_dtype, name):
    te, nv, nu = tables
    rows, k = x.shape
    n_tiles = w.shape[-1] // (n_w * tn)
    clamp = lambda m, nu_ref: jnp.minimum(m, nu_ref[0] - 1)
    w_specs = [pl.BlockSpec((1, k, tn),
                            lambda n, m, te_r, nv_r, nu_r, off=i * n_tiles: (te_r[clamp(m, nu_r)], 0, n + off))
               for i in range(n_w)]
    return pl.pallas_call(
        functools.partial(_grouped_kernel, tile_fn=tile_fn),
        grid_spec=pltpu.PrefetchScalarGridSpec(
            num_scalar_prefetch=3,
            grid=(n_tiles, rows // tm),
            in_specs=[pl.BlockSpec((tm, k), lambda n, m, te_r, nv_r, nu_r: (clamp(m, nu_r), 0))] + w_specs,
            out_specs=pl.BlockSpec((tm, tn), lambda n, m, te_r, nv_r, nu_r: (m, n)),
            scratch_shapes=[pltpu.VMEM((k, tn), BF16)] * n_w),
        out_shape=jax.ShapeDtypeStruct((rows, n_tiles * tn), out_dtype),
        compiler_params=_cparams(("arbitrary", "arbitrary"), V7X_VMEM_LIMIT_GROUPED),
        name=name,
    )(te, nv, nu, x, *([w] * n_w))


def _grouped_gu(x, w_gu, tables, tm, tn):
    return _grouped_matmul(_swiglu_tile, x, w_gu, tables, tm, tn, 2, BF16, "grouped_gate_up")


def _grouped_down(x, w_down, tables, tm, tn):
    return _grouped_matmul(_down_tile, x, w_down, tables, tm, tn, 1, F32, "grouped_down")


def _post_kernel(y_ref, x_ref, gate_ref, g_ref, o_ref):
    o_ref[...] = _post_residual(x_ref[...], y_ref[...], gate_ref[0], g_ref[...])


def _post(y, x, gate, g):
    tm = TM_NORM
    per_b = SEQ // tm
    return pl.pallas_call(
        _post_kernel,
        grid=(TOKENS // tm,),
        in_specs=[pl.BlockSpec((tm, D_MODEL), lambda i: (i, 0)),
                  pl.BlockSpec((tm, D_MODEL), lambda i: (i, 0)),
                  pl.BlockSpec((1, 1, D_MODEL), lambda i: (i // per_b, 0, 0)),
                  pl.BlockSpec((1, D_MODEL), lambda i: (0, 0))],
        out_specs=pl.BlockSpec((tm, D_MODEL), lambda i: (i, 0)),
        out_shape=jax.ShapeDtypeStruct((TOKENS, D_MODEL), F32),
        compiler_params=_cparams(("parallel",), V7X_VMEM_LIMIT),
        name="post_norm_residual",
    )(y, x, gate, g.reshape(1, D_MODEL))


def _row_copy(src_hbm, row, dst, slot, sem):
    return pltpu.make_async_copy(src_hbm.at[pl.ds(row, 1)], dst.at[pl.ds(slot, 1)], sem)


def _dispatch_kernel(src_ref, nu_ref, h_hbm, o_ref, buf, sem):
    i = pl.program_id(0)
    tm = buf.shape[0]

    @pl.when(i < nu_ref[0])
    def _():
        base = i * tm

        def issue(pair, c):
            for q in range(N_DMA_QUEUES):
                r = pair * N_DMA_QUEUES + q
                _row_copy(h_hbm, src_ref[base + r], buf, r, sem).start(priority=q)
            return c

        def wait(r, c):
            _row_copy(h_hbm, 0, buf, r, sem).wait()
            return c

        lax.fori_loop(0, tm // N_DMA_QUEUES, issue, 0)
        lax.fori_loop(0, tm, wait, 0)
        o_ref[...] = buf[...].astype(o_ref.dtype)

    @pl.when(i >= nu_ref[0])
    def _():
        o_ref[...] = jnp.zeros(o_ref.shape, o_ref.dtype)


def _dispatch(h, src, nu):
    tm = TM_DISPATCH
    return pl.pallas_call(
        _dispatch_kernel,
        grid_spec=pltpu.PrefetchScalarGridSpec(
            num_scalar_prefetch=2,
            grid=(N_SORT_PAD // tm,),
            in_specs=[pl.BlockSpec(memory_space=pl.ANY)],
            out_specs=pl.BlockSpec((tm, D_MODEL), lambda i, s_r, nu_r: (i, 0)),
            scratch_shapes=[pltpu.VMEM((tm, D_MODEL), F32), pltpu.SemaphoreType.DMA(())]),
        out_shape=jax.ShapeDtypeStruct((N_SORT_PAD, D_MODEL), BF16),
        compiler_params=_cparams(("arbitrary",)),
        name="moe_dispatch",
    )(src, nu, h)


def _combine_kernel(p0_ref, p1_ref, ys_hbm, wt_ref, x_ref, gate_ref, g_ref, o_ref, buf0, buf1, sem):
    i = pl.program_id(0)
    tm = buf0.shape[0]
    base = i * tm

    def issue(r, c):
        _row_copy(ys_hbm, p0_ref[base + r], buf0, r, sem).start(priority=0)
        _row_copy(ys_hbm, p1_ref[base + r], buf1, r, sem).start(priority=1)
        return c

    def wait(r, c):
        _row_copy(ys_hbm, 0, buf0, r, sem).wait()
        _row_copy(ys_hbm, 0, buf1, r, sem).wait()
        return c

    lax.fori_loop(0, tm, issue, 0)
    lax.fori_loop(0, tm, wait, 0)
    wt = wt_ref[...]
    y = wt[:, 0:1] * buf0[...] + wt[:, 1:2] * buf1[...]
    o_ref[...] = _post_residual(x_ref[...], y, gate_ref[0], g_ref[...])


def _combine(ys, pos0, pos1, wt, x, gate, g):
    tm = TM_COMBINE
    per_b = SEQ // tm
    return pl.pallas_call(
        _combine_kernel,
        grid_spec=pltpu.PrefetchScalarGridSpec(
            num_scalar_prefetch=2,
            grid=(TOKENS // tm,),
            in_specs=[pl.BlockSpec(memory_space=pl.ANY),
                      pl.BlockSpec((tm, TOP_K), lambda i, a, b: (i, 0)),
                      pl.BlockSpec((tm, D_MODEL), lambda i, a, b: (i, 0)),
                      pl.BlockSpec((1, 1, D_MODEL), lambda i, a, b: (i // per_b, 0, 0)),
                      pl.BlockSpec((1, D_MODEL), lambda i, a, b: (0, 0))],
            out_specs=pl.BlockSpec((tm, D_MODEL), lambda i, a, b: (i, 0)),
            scratch_shapes=[pltpu.VMEM((tm, D_MODEL), F32), pltpu.VMEM((tm, D_MODEL), F32),
                            pltpu.SemaphoreType.DMA(())]),
        out_shape=jax.ShapeDtypeStruct((TOKENS, D_MODEL), F32),
        compiler_params=_cparams(("arbitrary",)),
        name="moe_combine",
    )(pos0, pos1, ys, wt, x, gate, g.reshape(1, D_MODEL))


def _routing_tables(top_i, rank, counts):
    tm = TM_MOE
    n_tiles = N_SORT_PAD // tm
    experts = jnp.arange(N_EXPERTS, dtype=jnp.int32)
    padded = ((counts + tm - 1) // tm) * tm
    ends = jnp.cumsum(padded)
    offs = ends - padded
    group_start = jnp.sum((top_i[:, :, None] == experts) * offs, axis=-1)
    pos = (group_start + rank).astype(jnp.int32)
    tok = jnp.tile(jnp.arange(TOKENS, dtype=jnp.int32), TOP_K)
    src = jnp.zeros((N_SORT_PAD,), jnp.int32).at[pos.reshape(-1)].set(tok)
    tile_start = jnp.arange(n_tiles, dtype=jnp.int32) * tm
    te = jnp.sum((tile_start[:, None] >= ends[None, :]).astype(jnp.int32), axis=1)
    te = jnp.minimum(te, N_EXPERTS - 1)
    nu = (ends[-1] // tm).astype(jnp.int32).reshape(1)
    nv = jnp.clip((offs + counts)[te] - tile_start, 0, tm).astype(jnp.int32)
    return src, (te, nv, nu), pos[0], pos[1]


def _split_w_in(w):
    lat = MLA_Q_RANK + MLA_KV_RANK + MLA_ROPE
    w_lat = jnp.zeros((D_MODEL, TN_IN), BF16).at[:, :lat].set(w[:, :lat].astype(BF16))
    return w_lat, w[:, lat:].astype(BF16)


def _pad_w_q_up(w):
    w = w.reshape(MLA_Q_RANK, MLA_HEADS, MLA_NOPE + MLA_ROPE)
    pad = jnp.zeros((MLA_Q_RANK, MLA_HEADS, MLA_QK_PAD - MLA_NOPE - MLA_ROPE), w.dtype)
    return jnp.concatenate([w, pad], axis=2).reshape(MLA_Q_RANK, MLA_HEADS * MLA_QK_PAD).astype(BF16)


def _perm_w_kv_up(w):
    w = w.reshape(MLA_KV_RANK, MLA_HEADS, MLA_NOPE + MLA_V)
    k = w[:, :, :MLA_NOPE].reshape(MLA_KV_RANK, MLA_HEADS * MLA_NOPE)
    v = w[:, :, MLA_NOPE:].reshape(MLA_KV_RANK, MLA_HEADS * MLA_V)
    return jnp.concatenate([k, v], axis=1).astype(BF16)


def _one_group(tm):
    n_tiles = TOKENS // tm
    return jnp.zeros((n_tiles,), jnp.int32), jnp.full((n_tiles,), tm, jnp.int32), jnp.full((1,), n_tiles, jnp.int32)


def kernel(x, c, positions, ada_w, ada_b, norm_g, w_in, q_norm_g, kv_norm_g, w_q_up, w_kv_up, diff_lambda,
           diff_subln_g, w_branch_a, w_branch_b, b_gate, w_out, dense_w_gu, dense_w_down, router_w, moe_w_gu,
           moe_w_down):
    assert x.shape == (BATCH, SEQ, D_MODEL) and c.shape == (BATCH, D_MODEL)
    assert w_in.shape == (DEPTH, D_MODEL, 8000)
    xt = x.reshape(TOKENS, D_MODEL)
    cos, sin = _rope_tables(positions)
    mod = _adaln(c, ada_w, ada_b)

    for l in range(DEPTH):
        shift, scale, gate = (mod[l, 0, :, :, i * D_MODEL:(i + 1) * D_MODEL] for i in range(3))
        h = _norm_mod(xt, norm_g[l, 0], scale, shift)
        proj = _in_proj(h, *_split_w_in(w_in[l]), cos, sin)
        qa, ka, va = _mla_up(proj, q_norm_g[l], kv_norm_g[l], _pad_w_q_up(w_q_up[l]), _perm_w_kv_up(w_kv_up[l]),
                             cos, sin)
        ya = _mla_attn(qa, ka, va)
        yb = _diff_attn(proj, diff_lambda[l], diff_subln_g[l], l)
        xt = _merge_out(ya, yb, proj, b_gate[l], w_branch_a[l].astype(BF16), w_branch_b[l].astype(BF16),
                        w_out[l].astype(BF16), xt, gate, norm_g[l, 1])

        shift, scale, gate = (mod[l, 1, :, :, i * D_MODEL:(i + 1) * D_MODEL] for i in range(3))
        if l % 2 == 0:
            h = _norm_mod(xt, norm_g[l, 2], scale, shift)
            act = _grouped_gu(h, dense_w_gu[l // 2][None], _one_group(TM_DENSE_GU), TM_DENSE_GU, TN_DENSE_GU)
            y = _grouped_down(act, dense_w_down[l // 2][None], _one_group(TM_DENSE_DOWN), TM_DENSE_DOWN,
                              TN_DENSE_DOWN)
            xt = _post(y, xt, gate, norm_g[l, 3])
        else:
            h, top_i, top_w, rank, cnt = _norm_mod_route(xt, norm_g[l, 2], scale, shift, router_w[l // 2])
            src, tables, pos0, pos1 = _routing_tables(top_i, rank, cnt[:, 0].astype(jnp.int32))
            xs = _dispatch(h, src, tables[2] * (TM_MOE // TM_DISPATCH))
            act = _grouped_gu(xs, moe_w_gu[l // 2], tables, TM_MOE, TN_MOE_GU)
            ys = _grouped_down(act, moe_w_down[l // 2], tables, TM_MOE, TN_MOE_DOWN)
            xt = _combine(ys, pos0, pos1, top_w.T, xt, gate, norm_g[l, 3])
    return xt.reshape(BATCH, SEQ, D_MODEL)
```

```python
import functools
import math

import jax
import jax.numpy as jnp
import numpy as np
from jax import lax
from jax.experimental import pallas as pl
from jax.experimental.pallas import tpu as pltpu

F32 = jnp.float32
BF16 = jnp.bfloat16

D_MODEL = 2048
BATCH = 2
SEQ = 4096
TOKENS = BATCH * SEQ
DEPTH = 2
CHUNK = 64
ROPE_THETA = 10000.0
EPS = 1e-6
NEG_INF = -1e30

MLA_HEADS = 8
MLA_Q_RANK = 512
MLA_KV_RANK = 256
MLA_NOPE = 128
MLA_ROPE = 64
MLA_V = 128
MLA_QK_PAD = 256

DIFF_HEADS = 8
DIFF_HEAD_DIM = 64

D_FF_DENSE = 5632
N_EXPERTS = 8
TOP_K = 2
D_FF_EXPERT = 7168

D_IN_PAD = 8192

LANES = 128
N_DMA_QUEUES = 2
V7X_VMEM_LIMIT = 56 * 1024 * 1024
V7X_VMEM_LIMIT_GROUPED = 60 * 1024 * 1024

TM_NORM = 512
TM_IN = 2048
TN_IN = 1024
TM_UP = 512
TQ = 1024
TK = 512
TM_MERGE = 256
TM_DENSE_GU = 1024
TM_DENSE_DOWN = 512
TN_DENSE_GU = 512
TN_DENSE_DOWN = 512
MOE_GROUP_PAD = 1024
MOE_ROW_QUANTUM = 256
TM_MOE_GU = 1024
TM_MOE_DOWN = 512
TM_DISPATCH = 256
TN_MOE_GU = 512
TN_MOE_DOWN = 512
N_SORT_PAD = TOP_K * TOKENS + N_EXPERTS * MOE_GROUP_PAD
TM_COMBINE = 256


def _cparams(sem, vmem=None):
    return pltpu.CompilerParams(dimension_semantics=sem, vmem_limit_bytes=vmem)


def _nt_dot(a, b):
    return lax.dot_general(a, b, (((1,), (1,)), ((), ())), preferred_element_type=F32)


def _rms(y):
    return y * lax.rsqrt(jnp.mean(y * y, axis=-1, keepdims=True) + EPS)


def _rope_lanes(blk, cos, sin_signed, first_half):
    rot = jnp.where(first_half, pltpu.roll(blk, 96, 1), pltpu.roll(blk, 32, 1))
    return blk * cos + rot * sin_signed


def _first_half_mask(rows):
    lane = lax.broadcasted_iota(jnp.int32, (rows, LANES), 1)
    return (lane % 64) < 32


def _rope_tables_kernel(pos_ref, freq_ref, sign_ref, cos_ref, sin_ref):
    ang = pos_ref[...].astype(F32) * freq_ref[...]
    cos_ref[...] = jnp.cos(ang)
    sin_ref[...] = jnp.sin(ang) * sign_ref[...]


def _rope_tables(positions):
    half = MLA_ROPE // 2
    inv_freq = ROPE_THETA ** (-jnp.arange(0, MLA_ROPE, 2, dtype=F32) / MLA_ROPE)
    freq = jnp.tile(inv_freq, LANES // half).reshape(1, LANES)
    sign = jnp.tile(jnp.concatenate([-jnp.ones((half,), F32), jnp.ones((half,), F32)]), LANES // MLA_ROPE)
    sign = sign.reshape(1, LANES)
    pos = positions.reshape(TOKENS, 1)
    tm = 1024
    return pl.pallas_call(
        _rope_tables_kernel,
        grid=(TOKENS // tm,),
        in_specs=[pl.BlockSpec((tm, 1), lambda i: (i, 0)),
                  pl.BlockSpec((1, LANES), lambda i: (0, 0)),
                  pl.BlockSpec((1, LANES), lambda i: (0, 0))],
        out_specs=[pl.BlockSpec((tm, LANES), lambda i: (i, 0)),
                   pl.BlockSpec((tm, LANES), lambda i: (i, 0))],
        out_shape=[jax.ShapeDtypeStruct((TOKENS, LANES), F32)] * 2,
        compiler_params=_cparams(("parallel",)),
        name="rope_tables",
    )(pos, freq, sign)


def _adaln_kernel(c_ref, w_ref, b_ref, o_ref):
    @pl.when(pl.program_id(1) == 0)
    def _():
        o_ref[0] = jnp.broadcast_to(b_ref[0], o_ref.shape[1:])

    c = c_ref[...]
    cs = (c * jax.nn.sigmoid(c)).astype(BF16)
    o_ref[0] += jnp.dot(cs, w_ref[0].astype(BF16), preferred_element_type=F32)


def _adaln(c, ada_w, ada_b):
    n_mod = DEPTH * 2
    w = ada_w.reshape(n_mod, D_MODEL, 3 * D_MODEL)
    b = ada_b.reshape(n_mod, 1, 3 * D_MODEL)
    c8 = jnp.zeros((8, D_MODEL), F32).at[:BATCH].set(c)
    tk = 256
    out = pl.pallas_call(
        _adaln_kernel,
        grid=(n_mod, D_MODEL // tk),
        in_specs=[pl.BlockSpec((8, tk), lambda m, k: (0, k)),
                  pl.BlockSpec((1, tk, 3 * D_MODEL), lambda m, k: (m, k, 0)),
                  pl.BlockSpec((1, 1, 3 * D_MODEL), lambda m, k: (m, 0, 0))],
        out_specs=pl.BlockSpec((1, 8, 3 * D_MODEL), lambda m, k: (m, 0, 0)),
        out_shape=jax.ShapeDtypeStruct((n_mod, 8, 3 * D_MODEL), F32),
        compiler_params=_cparams(("parallel", "arbitrary"), V7X_VMEM_LIMIT),
        name="adaln",
    )(c8, w, b)
    mod = out[:, :BATCH, :].reshape(DEPTH, 2, BATCH, 1, 3 * D_MODEL)
    return mod


def _norm_mod_kernel(x_ref, g_ref, sc_ref, sh_ref, h_ref):
    h = _rms(x_ref[...]) * g_ref[...] * (1.0 + sc_ref[0]) + sh_ref[0]
    h_ref[...] = h.astype(h_ref.dtype)


def _norm_mod_route_kernel(x_ref, g_ref, sc_ref, sh_ref, rwh_ref, rwl_ref, h_ref, idx_ref, wt_ref, rank_ref,
                           cnt_ref, cnt_s):
    @pl.when(pl.program_id(0) == 0)
    def _():
        cnt_s[...] = jnp.zeros(cnt_s.shape, F32)

    h = _rms(x_ref[...]) * g_ref[...] * (1.0 + sc_ref[0]) + sh_ref[0]
    h_ref[...] = h
    h_hi = h.astype(BF16)
    h_lo = (h - h_hi.astype(F32)).astype(BF16)
    logits = (jnp.dot(h_hi, rwh_ref[...], preferred_element_type=F32)
              + jnp.dot(h_hi, rwl_ref[...], preferred_element_type=F32)
              + jnp.dot(h_lo, rwh_ref[...], preferred_element_type=F32))
    lt = logits.T[:N_EXPERTS, :]
    eid = lax.broadcasted_iota(jnp.int32, lt.shape, 0)
    m1 = jnp.max(lt, axis=0, keepdims=True)
    i1 = jnp.min(jnp.where(lt == m1, eid, N_EXPERTS), axis=0, keepdims=True)
    lt2 = jnp.where(eid == i1, -jnp.inf, lt)
    m2 = jnp.max(lt2, axis=0, keepdims=True)
    i2 = jnp.min(jnp.where(lt2 == m2, eid, N_EXPERTS), axis=0, keepdims=True)
    e2 = jnp.exp(m2 - m1)
    w1 = 1.0 / (1.0 + e2)
    idx_ref[0:1, :] = i1
    idx_ref[1:2, :] = i2
    wt_ref[0:1, :] = w1
    wt_ref[1:2, :] = e2 * w1
    tm = lt.shape[1]
    chosen = jnp.logical_or(eid == i1, eid == i2)
    upper = (lax.broadcasted_iota(jnp.int32, (tm, tm), 0) < lax.broadcasted_iota(jnp.int32, (tm, tm), 1))
    before = jnp.dot(chosen.astype(BF16), upper.astype(BF16), preferred_element_type=F32) + cnt_s[:, 0:1]
    rank_ref[0:1, :] = jnp.sum(jnp.where(eid == i1, before, 0.0), axis=0, keepdims=True).astype(jnp.int32)
    rank_ref[1:2, :] = jnp.sum(jnp.where(eid == i2, before, 0.0), axis=0, keepdims=True).astype(jnp.int32)
    cnt_s[...] = cnt_s[...] + jnp.sum(chosen.astype(F32), axis=1, keepdims=True)
    cnt_ref[...] = cnt_s[...]


def _mod_specs(tm):
    per_b = SEQ // tm
    return [pl.BlockSpec((tm, D_MODEL), lambda i: (i, 0)),
            pl.BlockSpec((1, D_MODEL), lambda i: (0, 0)),
            pl.BlockSpec((1, 1, D_MODEL), lambda i: (i // per_b, 0, 0)),
            pl.BlockSpec((1, 1, D_MODEL), lambda i: (i // per_b, 0, 0))]


def _norm_mod(x, g, scale, shift):
    tm = TM_NORM
    return pl.pallas_call(
        _norm_mod_kernel,
        grid=(TOKENS // tm,),
        in_specs=_mod_specs(tm),
        out_specs=pl.BlockSpec((tm, D_MODEL), lambda i: (i, 0)),
        out_shape=jax.ShapeDtypeStruct((TOKENS, D_MODEL), BF16),
        compiler_params=_cparams(("parallel",)),
        name="norm_mod",
    )(x, g.reshape(1, D_MODEL), scale, shift)


def _norm_mod_route(x, g, scale, shift, router_w):
    tm = TM_NORM
    rw = jnp.zeros((D_MODEL, LANES), F32).at[:, :N_EXPERTS].set(router_w)
    rw_hi = rw.astype(BF16)
    rw_lo = (rw - rw_hi.astype(F32)).astype(BF16)
    return pl.pallas_call(
        _norm_mod_route_kernel,
        grid=(TOKENS // tm,),
        in_specs=_mod_specs(tm) + [pl.BlockSpec((D_MODEL, LANES), lambda i: (0, 0)),
                                   pl.BlockSpec((D_MODEL, LANES), lambda i: (0, 0))],
        out_specs=[pl.BlockSpec((tm, D_MODEL), lambda i: (i, 0)),
                   pl.BlockSpec((TOP_K, tm), lambda i: (0, i)),
                   pl.BlockSpec((TOP_K, tm), lambda i: (0, i)),
                   pl.BlockSpec((TOP_K, tm), lambda i: (0, i)),
                   pl.BlockSpec((N_EXPERTS, LANES), lambda i: (0, 0))],
        out_shape=[jax.ShapeDtypeStruct((TOKENS, D_MODEL), F32),
                   jax.ShapeDtypeStruct((TOP_K, TOKENS), jnp.int32),
                   jax.ShapeDtypeStruct((TOP_K, TOKENS), F32),
                   jax.ShapeDtypeStruct((TOP_K, TOKENS), jnp.int32),
                   jax.ShapeDtypeStruct((N_EXPERTS, LANES), F32)],
        scratch_shapes=[pltpu.VMEM((N_EXPERTS, LANES), F32)],
        compiler_params=_cparams(("arbitrary",)),
        name="norm_mod_route",
    )(x, g.reshape(1, D_MODEL), scale, shift, rw_hi, rw_lo)


def _in_proj_kernel(h_ref, wlat_ref, wrest_ref, cos_ref, sin_ref, o_ref):
    n = pl.program_id(0)
    tm, tn = o_ref.shape

    def project(w_ref):
        return jnp.dot(h_ref[...], w_ref[...], preferred_element_type=F32)

    def rope_cols(acc, lo, hi, mult):
        fh = _first_half_mask(tm)
        cos, sin = cos_ref[...], sin_ref[...]
        for c in range(lo // LANES, hi // LANES):
            blk = acc[:, c * LANES:(c + 1) * LANES]
            o_ref[:, c * LANES:(c + 1) * LANES] = (_rope_lanes(blk, cos, sin, fh) * mult).astype(o_ref.dtype)

    @pl.when(n == 0)
    def _():
        acc = project(wlat_ref)
        lat = MLA_Q_RANK + MLA_KV_RANK
        o_ref[:, :lat] = acc[:, :lat].astype(o_ref.dtype)
        rope_cols(acc, lat, tn, 1.0)

    @pl.when(n == 1)
    def _():
        rope_cols(project(wrest_ref), 0, tn, DIFF_HEAD_DIM ** -0.5)

    @pl.when(n == 2)
    def _():
        rope_cols(project(wrest_ref), 0, tn, 1.0)

    @pl.when(n >= 3)
    def _():
        o_ref[...] = project(wrest_ref).astype(o_ref.dtype)


def _in_proj(h, w_lat, w_rest, cos, sin):
    tm, tn = TM_IN, TN_IN
    return pl.pallas_call(
        _in_proj_kernel,
        grid=(D_IN_PAD // tn, TOKENS // tm),
        in_specs=[pl.BlockSpec((tm, D_MODEL), lambda n, m: (m, 0)),
                  pl.BlockSpec((D_MODEL, tn), lambda n, m: (0, 0)),
                  pl.BlockSpec((D_MODEL, tn), lambda n, m: (0, jnp.maximum(n - 1, 0))),
                  pl.BlockSpec((tm, LANES), lambda n, m: (m, 0)),
                  pl.BlockSpec((tm, LANES), lambda n, m: (m, 0))],
        out_specs=pl.BlockSpec((tm, tn), lambda n, m: (m, n)),
        out_shape=jax.ShapeDtypeStruct((TOKENS, D_IN_PAD), BF16),
        compiler_params=_cparams(("parallel", "parallel"), V7X_VMEM_LIMIT),
        name="in_proj",
    )(h, w_lat, w_rest, cos, sin)


def _mla_up_kernel(lat_ref, qg_ref, kvg_ref, wq_ref, wkv_ref, cos_ref, sin_ref, q_ref, k_ref, v_ref):
    tm = lat_ref.shape[0]
    scale = (MLA_NOPE + MLA_ROPE) ** -0.5
    q_lat = lat_ref[:, :MLA_Q_RANK].astype(F32)
    kv_lat = lat_ref[:, MLA_Q_RANK:MLA_Q_RANK + MLA_KV_RANK].astype(F32)
    k_pe = lat_ref[:, MLA_Q_RANK + MLA_KV_RANK:MLA_Q_RANK + MLA_KV_RANK + LANES]
    qn = (_rms(q_lat) * qg_ref[...]).astype(BF16)
    kvn = (_rms(kv_lat) * kvg_ref[...]).astype(BF16)
    q = jnp.dot(qn, wq_ref[...], preferred_element_type=F32)
    kv = jnp.dot(kvn, wkv_ref[...], preferred_element_type=F32)
    fh = _first_half_mask(tm)
    cos, sin = cos_ref[...], sin_ref[...]
    for h in range(MLA_HEADS):
        c0 = h * MLA_QK_PAD
        q_ref[:, c0:c0 + LANES] = (q[:, c0:c0 + LANES] * scale).astype(BF16)
        q_pe = _rope_lanes(q[:, c0 + LANES:c0 + 2 * LANES], cos, sin, fh)
        q_ref[:, c0 + LANES:c0 + 2 * LANES] = (q_pe * scale).astype(BF16)
        k_ref[:, c0:c0 + LANES] = kv[:, h * LANES:(h + 1) * LANES].astype(BF16)
        k_ref[:, c0 + LANES:c0 + 2 * LANES] = k_pe
    v_ref[...] = kv[:, MLA_HEADS * MLA_NOPE:].astype(BF16)


def _mla_up(proj, q_norm_g, kv_norm_g, wq_p, wkv_p, cos, sin):
    tm = TM_UP
    lat_w = 1024
    hq = MLA_HEADS * MLA_QK_PAD
    hv = MLA_HEADS * MLA_V
    const = lambda i: (0, 0)
    return pl.pallas_call(
        _mla_up_kernel,
        grid=(TOKENS // tm,),
        in_specs=[pl.BlockSpec((tm, lat_w), lambda i: (i, 0)),
                  pl.BlockSpec((1, MLA_Q_RANK), const),
                  pl.BlockSpec((1, MLA_KV_RANK), const),
                  pl.BlockSpec((MLA_Q_RANK, hq), const),
                  pl.BlockSpec((MLA_KV_RANK, 2 * hv), const),
                  pl.BlockSpec((tm, LANES), lambda i: (i, 0)),
                  pl.BlockSpec((tm, LANES), lambda i: (i, 0))],
        out_specs=[pl.BlockSpec((tm, hq), lambda i: (i, 0)),
                   pl.BlockSpec((tm, hq), lambda i: (i, 0)),
                   pl.BlockSpec((tm, hv), lambda i: (i, 0))],
        out_shape=[jax.ShapeDtypeStruct((TOKENS, hq), BF16),
                   jax.ShapeDtypeStruct((TOKENS, hq), BF16),
                   jax.ShapeDtypeStruct((TOKENS, hv), BF16)],
        compiler_params=_cparams(("parallel",), V7X_VMEM_LIMIT),
        name="mla_up",
    )(proj, q_norm_g.reshape(1, -1), kv_norm_g.reshape(1, -1), wq_p, wkv_p, cos, sin)


def _diag_masked(s):
    qc = lax.broadcasted_iota(jnp.int32, s.shape, 0) // CHUNK
    kc = lax.broadcasted_iota(jnp.int32, s.shape, 1) // CHUNK
    return jnp.where(kc <= qc, s, NEG_INF)


def _key_rows(j):
    return pl.ds(pl.multiple_of(j * TK, TK), TK)


def _softmax_pv(s, vaug, m_s, acc_s, row0):
    tk = s.shape[1]
    m_prev = m_s[row0:, :]
    m_new = jnp.maximum(m_prev, jnp.max(s, axis=1, keepdims=True))
    alpha = jnp.exp(m_prev - m_new)
    p = jnp.exp(s - jnp.tile(m_new, (1, tk // LANES)))
    pv = jnp.dot(p.astype(BF16), vaug, preferred_element_type=F32)
    acc_s[row0:, :] = jnp.tile(alpha, (1, 2)) * acc_s[row0:, :] + pv
    m_s[row0:, :] = m_new


def _attn_sweep(qi, scores, vaug_s, states):
    n_diag = TQ // TK
    n_full = qi * n_diag

    def consume(ss, j, row0):
        vaug = vaug_s[_key_rows(j), :]
        for s, (m_s, acc_s) in zip(ss, states):
            _softmax_pv(s, vaug, m_s, acc_s, row0)

    d = n_diag - 1
    pending, idx, row0 = scores(n_full + d, d * TK), n_full + d, d * TK
    for d in reversed(range(n_diag - 1)):
        nxt = scores(n_full + d, d * TK)
        consume(pending, idx, row0)
        pending, idx, row0 = nxt, n_full + d, d * TK

    def body(j, carry):
        nxt = scores(j, None)
        consume(*carry, 0)
        return nxt, j

    consume(*lax.fori_loop(0, n_full, body, (pending, idx)), 0)


def _init_attn_state(v_ref, vaug_s, states):
    @pl.when(pl.program_id(2) == 0)
    def _():
        vaug_s[:, :LANES] = v_ref[...]
        vaug_s[:, LANES:] = jnp.ones((vaug_s.shape[0], LANES), vaug_s.dtype)

    for m_s, acc_s in states:
        m_s[...] = jnp.full(m_s.shape, -jnp.inf, F32)
        acc_s[...] = jnp.zeros(acc_s.shape, F32)


def _attn_scratch(n_states):
    return ([pltpu.VMEM((SEQ, 2 * LANES), BF16)]
            + [pltpu.VMEM((TQ, LANES), F32), pltpu.VMEM((TQ, 2 * LANES), F32)] * n_states)


def _normalised(acc_s):
    return acc_s[:, :LANES] * (1.0 / acc_s[:, LANES:])


def _mla_attn_kernel(q_ref, k_ref, v_ref, o_ref, vaug_s, m_s, acc_s):
    qi = pl.program_id(2)
    states = ((m_s, acc_s),)
    _init_attn_state(v_ref, vaug_s, states)

    def scores(j, row0):
        k = k_ref[_key_rows(j), :]
        if row0 is None:
            return (_nt_dot(q_ref[...], k),)
        return (_diag_masked(_nt_dot(q_ref[row0:, :], k)),)

    _attn_sweep(qi, scores, vaug_s, states)
    o_ref[...] = _normalised(acc_s).astype(o_ref.dtype)


def _mla_attn(qa, ka, va):
    tq = TQ
    nq = SEQ // tq
    return pl.pallas_call(
        _mla_attn_kernel,
        grid=(BATCH, MLA_HEADS, nq),
        in_specs=[pl.BlockSpec((tq, MLA_QK_PAD), lambda b, h, i: (b * nq + i, h)),
                  pl.BlockSpec((SEQ, MLA_QK_PAD), lambda b, h, i: (b, h)),
                  pl.BlockSpec((SEQ, MLA_V), lambda b, h, i: (b, h))],
        out_specs=pl.BlockSpec((tq, MLA_V), lambda b, h, i: (b * nq + i, h)),
        out_shape=jax.ShapeDtypeStruct((TOKENS, MLA_HEADS * MLA_V), BF16),
        scratch_shapes=_attn_scratch(1),
        compiler_params=_cparams(("parallel", "parallel", "arbitrary"), V7X_VMEM_LIMIT),
        name="mla_attn",
    )(qa, ka, va)


def _diff_attn_kernel(q_ref, k_ref, v_ref, lam_ref, g_ref, o_ref, vaug_s, m1, a1, m2, a2, *, lam_init):
    qi = pl.program_id(2)
    states = ((m1, a1), (m2, a2))
    _init_attn_state(v_ref, vaug_s, states)
    q = q_ref[...]
    lane = lax.broadcasted_iota(jnp.int32, q.shape, 1)
    zero = jnp.zeros_like(q)
    q1 = jnp.where(lane < DIFF_HEAD_DIM, q, zero)
    q2 = jnp.where(lane >= DIFF_HEAD_DIM, q, zero)

    def scores(j, row0):
        k = k_ref[_key_rows(j), :]
        if row0 is None:
            return (_nt_dot(q1, k), _nt_dot(q2, k))
        return (_diag_masked(_nt_dot(q1[row0:, :], k)), _diag_masked(_nt_dot(q2[row0:, :], k)))

    _attn_sweep(qi, scores, vaug_s, states)
    lp = lam_ref[...]
    lam = (jnp.exp(jnp.sum(lp[0:1] * lp[1:2], axis=1, keepdims=True))
           - jnp.exp(jnp.sum(lp[2:3] * lp[3:4], axis=1, keepdims=True)) + lam_init)
    od = _normalised(a1) - lam * _normalised(a2)
    o_ref[...] = (_rms(od) * g_ref[...] * (1.0 - lam_init)).astype(o_ref.dtype)


def _diff_attn(proj, diff_lambda, subln_g, layer):
    tq = TQ
    nq = SEQ // tq
    hd = 2 * DIFF_HEAD_DIM
    q0, k0, v0 = 1024 // hd, 2048 // hd, 3072 // hd
    lam_init = 0.8 - 0.6 * math.exp(-0.3 * layer)
    return pl.pallas_call(
        functools.partial(_diff_attn_kernel, lam_init=lam_init),
        grid=(BATCH, DIFF_HEADS, nq),
        in_specs=[pl.BlockSpec((tq, hd), lambda b, h, i: (b * nq + i, q0 + h)),
                  pl.BlockSpec((SEQ, hd), lambda b, h, i: (b, k0 + h)),
                  pl.BlockSpec((SEQ, hd), lambda b, h, i: (b, v0 + h)),
                  pl.BlockSpec((4, DIFF_HEAD_DIM), lambda b, h, i: (0, 0)),
                  pl.BlockSpec((1, hd), lambda b, h, i: (0, 0))],
        out_specs=pl.BlockSpec((tq, hd), lambda b, h, i: (b * nq + i, h)),
        out_shape=jax.ShapeDtypeStruct((TOKENS, DIFF_HEADS * hd), BF16),
        scratch_shapes=_attn_scratch(2),
        compiler_params=_cparams(("parallel", "parallel", "arbitrary"), V7X_VMEM_LIMIT),
        name="diff_attn",
    )(proj, proj, proj, diff_lambda, subln_g.reshape(1, hd))


def _post_residual(x, y, gate, g):
    return x + gate * (_rms(y) * g)


def _merge_out_kernel(ya_ref, yb_ref, ga_ref, gb_ref, bg_ref, wa_ref, wb_ref, wo_ref, x_ref, gate_ref, g_ref,
                      o_ref):
    a = jnp.dot(ya_ref[...], wa_ref[...], preferred_element_type=F32)
    b = jnp.dot(yb_ref[...], wb_ref[...], preferred_element_type=F32)
    bg = bg_ref[...]
    merged = (jax.nn.sigmoid(ga_ref[...].astype(F32) + bg[0:1]) * a
              + jax.nn.sigmoid(gb_ref[...].astype(F32) + bg[1:2]) * b)
    y = jnp.dot(merged.astype(BF16), wo_ref[...], preferred_element_type=F32)
    o_ref[...] = _post_residual(x_ref[...], y, gate_ref[0], g_ref[...])


def _resident(shape):
    return pl.BlockSpec(shape, lambda i: (0,) * len(shape), pipeline_mode=pl.Buffered(1))


def _merge_out(ya, yb, proj, b_gate, wa, wb, wo, x, gate, g):
    tm = TM_MERGE
    per_b = SEQ // tm
    ga_blk, gb_blk = 4096 // D_MODEL, 6144 // D_MODEL
    return pl.pallas_call(
        _merge_out_kernel,
        grid=(TOKENS // tm,),
        in_specs=[pl.BlockSpec((tm, 1024), lambda i: (i, 0)),
                  pl.BlockSpec((tm, 1024), lambda i: (i, 0)),
                  pl.BlockSpec((tm, D_MODEL), lambda i: (i, ga_blk)),
                  pl.BlockSpec((tm, D_MODEL), lambda i: (i, gb_blk)),
                  _resident((2, D_MODEL)),
                  _resident((1024, D_MODEL)),
                  _resident((1024, D_MODEL)),
                  _resident((D_MODEL, D_MODEL)),
                  pl.BlockSpec((tm, D_MODEL), lambda i: (i, 0)),
                  pl.BlockSpec((1, 1, D_MODEL), lambda i: (i // per_b, 0, 0)),
                  _resident((1, D_MODEL))],
        out_specs=pl.BlockSpec((tm, D_MODEL), lambda i: (i, 0)),
        out_shape=jax.ShapeDtypeStruct((TOKENS, D_MODEL), F32),
        compiler_params=_cparams(("parallel",), V7X_VMEM_LIMIT),
        name="merge_out",
    )(ya, yb, proj, proj, b_gate, wa, wb, wo, x, gate, g.reshape(1, D_MODEL))


def _new_weights(te_ref, m):
    return jnp.logical_or(m == 0, te_ref[m] != te_ref[jnp.maximum(m - 1, 0)])


def _grouped_kernel(te_ref, nv_ref, nu_ref, x_ref, *refs, tile_fn):
    n_w = (len(refs) - 1) // 2
    w_refs, o_ref, w_bf16 = refs[:n_w], refs[n_w], refs[n_w + 1:]
    m = pl.program_id(1)
    tm = x_ref.shape[0]
    in_use = m < nu_ref[0]
    n_blocks = jnp.where(in_use, (nv_ref[m] + MOE_ROW_QUANTUM - 1) // MOE_ROW_QUANTUM, 0)

    @pl.when(jnp.logical_and(in_use, _new_weights(te_ref, m)))
    def _():
        for w_ref, w_s in zip(w_refs, w_bf16):
            w_s[...] = w_ref[0].astype(BF16)

    for k in range(1, tm // MOE_ROW_QUANTUM + 1):
        rows = k * MOE_ROW_QUANTUM

        @pl.when(n_blocks == k)
        def _(rows=rows):
            o_ref[:rows, :] = tile_fn(x_ref[:rows, :], *w_bf16).astype(o_ref.dtype)
            if rows < tm:
                o_ref[rows:, :] = jnp.zeros((tm - rows, o_ref.shape[1]), o_ref.dtype)

    @pl.when(n_blocks == 0)
    def _():
        o_ref[...] = jnp.zeros(o_ref.shape, o_ref.dtype)


def _swiglu_tile(x, wg_s, wu_s):
    g = jnp.dot(x, wg_s[...], preferred_element_type=F32)
    u = jnp.dot(x, wu_s[...], preferred_element_type=F32)
    return g * jax.nn.sigmoid(g) * u


def _down_tile(x, w_s):
    return jnp.dot(x, w_s[...], preferred_element_type=F32)


def _grouped_matmul(tile_fn, x, w, tables, tm, tn, n_w, out_dtype, name):
    te, nv, nu = tables
    rows, k = x.shape
    n_tiles = w.shape[-1] // (n_w * tn)
    clamp = lambda m, nu_ref: jnp.minimum(m, nu_ref[0] - 1)
    w_specs = [pl.BlockSpec((1, k, tn),
                            lambda n, m, te_r, nv_r, nu_r, off=i * n_tiles: (te_r[clamp(m, nu_r)], 0, n + off))
               for i in range(n_w)]
    return pl.pallas_call(
        functools.partial(_grouped_kernel, tile_fn=tile_fn),
        grid_spec=pltpu.PrefetchScalarGridSpec(
            num_scalar_prefetch=3,
            grid=(n_tiles, rows // tm),
            in_specs=[pl.BlockSpec((tm, k), lambda n, m, te_r, nv_r, nu_r: (clamp(m, nu_r), 0))] + w_specs,
            out_specs=pl.BlockSpec((tm, tn), lambda n, m, te_r, nv_r, nu_r: (m, n)),
            scratch_shapes=[pltpu.VMEM((k, tn), BF16)] * n_w),
        out_shape=jax.ShapeDtypeStruct((rows, n_tiles * tn), out_dtype),
        compiler_params=_cparams(("arbitrary", "arbitrary"), V7X_VMEM_LIMIT_GROUPED),
        name=name,
    )(te, nv, nu, x, *([w] * n_w))


def _grouped_gu(x, w_gu, tables, tm, tn):
    return _grouped_matmul(_swiglu_tile, x, w_gu, tables, tm, tn, 2, BF16, "grouped_gate_up")


def _grouped_down(x, w_down, tables, tm, tn):
    return _grouped_matmul(_down_tile, x, w_down, tables, tm, tn, 1, F32, "grouped_down")


def _post_kernel(y_ref, x_ref, gate_ref, g_ref, o_ref):
    o_ref[...] = _post_residual(x_ref[...], y_ref[...], gate_ref[0], g_ref[...])


def _post(y, x, gate, g):
    tm = TM_NORM
    per_b = SEQ // tm
    return pl.pallas_call(
        _post_kernel,
        grid=(TOKENS // tm,),
        in_specs=[pl.BlockSpec((tm, D_MODEL), lambda i: (i, 0)),
                  pl.BlockSpec((tm, D_MODEL), lambda i: (i, 0)),
                  pl.BlockSpec((1, 1, D_MODEL), lambda i: (i // per_b, 0, 0)),
                  pl.BlockSpec((1, D_MODEL), lambda i: (0, 0))],
        out_specs=pl.BlockSpec((tm, D_MODEL), lambda i: (i, 0)),
        out_shape=jax.ShapeDtypeStruct((TOKENS, D_MODEL), F32),
        compiler_params=_cparams(("parallel",), V7X_VMEM_LIMIT),
        name="post_norm_residual",
    )(y, x, gate, g.reshape(1, D_MODEL))


def _row_copy(src_hbm, row, dst, slot, sem):
    return pltpu.make_async_copy(src_hbm.at[pl.ds(row, 1)], dst.at[pl.ds(slot, 1)], sem)


def _dispatch_kernel(src_ref, nv_ref, h_hbm, o_ref, buf, sem):
    i = pl.program_id(0)
    tm = buf.shape[0]

    @pl.when(nv_ref[i] > 0)
    def _():
        base = i * tm

        def issue(pair, c):
            for q in range(N_DMA_QUEUES):
                r = pair * N_DMA_QUEUES + q
                _row_copy(h_hbm, src_ref[base + r], buf, r, sem).start(priority=q)
            return c

        def wait(r, c):
            _row_copy(h_hbm, 0, buf, r, sem).wait()
            return c

        lax.fori_loop(0, tm // N_DMA_QUEUES, issue, 0)
        lax.fori_loop(0, tm, wait, 0)
        o_ref[...] = buf[...].astype(o_ref.dtype)

    @pl.when(nv_ref[i] == 0)
    def _():
        o_ref[...] = jnp.zeros(o_ref.shape, o_ref.dtype)


def _dispatch(h, src, nv):
    tm = TM_DISPATCH
    return pl.pallas_call(
        _dispatch_kernel,
        grid_spec=pltpu.PrefetchScalarGridSpec(
            num_scalar_prefetch=2,
            grid=(N_SORT_PAD // tm,),
            in_specs=[pl.BlockSpec(memory_space=pl.ANY)],
            out_specs=pl.BlockSpec((tm, D_MODEL), lambda i, s_r, nv_r: (i, 0)),
            scratch_shapes=[pltpu.VMEM((tm, D_MODEL), F32), pltpu.SemaphoreType.DMA(())]),
        out_shape=jax.ShapeDtypeStruct((N_SORT_PAD, D_MODEL), BF16),
        compiler_params=_cparams(("arbitrary",)),
        name="moe_dispatch",
    )(src, nv, h)


def _combine_kernel(p0_ref, p1_ref, ys_hbm, wt_ref, x_ref, gate_ref, g_ref, o_ref, buf0, buf1, sem):
    i = pl.program_id(0)
    tm = buf0.shape[0]
    base = i * tm

    def issue(r, c):
        _row_copy(ys_hbm, p0_ref[base + r], buf0, r, sem).start(priority=0)
        _row_copy(ys_hbm, p1_ref[base + r], buf1, r, sem).start(priority=1)
        return c

    def wait(r, c):
        _row_copy(ys_hbm, 0, buf0, r, sem).wait()
        _row_copy(ys_hbm, 0, buf1, r, sem).wait()
        return c

    lax.fori_loop(0, tm, issue, 0)
    lax.fori_loop(0, tm, wait, 0)
    wt = wt_ref[...]
    y = wt[:, 0:1] * buf0[...] + wt[:, 1:2] * buf1[...]
    o_ref[...] = _post_residual(x_ref[...], y, gate_ref[0], g_ref[...])


def _combine(ys, pos0, pos1, wt, x, gate, g):
    tm = TM_COMBINE
    per_b = SEQ // tm
    return pl.pallas_call(
        _combine_kernel,
        grid_spec=pltpu.PrefetchScalarGridSpec(
            num_scalar_prefetch=2,
            grid=(TOKENS // tm,),
            in_specs=[pl.BlockSpec(memory_space=pl.ANY),
                      pl.BlockSpec((tm, TOP_K), lambda i, a, b: (i, 0)),
                      pl.BlockSpec((tm, D_MODEL), lambda i, a, b: (i, 0)),
                      pl.BlockSpec((1, 1, D_MODEL), lambda i, a, b: (i // per_b, 0, 0)),
                      pl.BlockSpec((1, D_MODEL), lambda i, a, b: (0, 0))],
            out_specs=pl.BlockSpec((tm, D_MODEL), lambda i, a, b: (i, 0)),
            scratch_shapes=[pltpu.VMEM((tm, D_MODEL), F32), pltpu.VMEM((tm, D_MODEL), F32),
                            pltpu.SemaphoreType.DMA(())]),
        out_shape=jax.ShapeDtypeStruct((TOKENS, D_MODEL), F32),
        compiler_params=_cparams(("arbitrary",)),
        name="moe_combine",
    )(pos0, pos1, ys, wt, x, gate, g.reshape(1, D_MODEL))


def _routing_tables(top_i, rank, counts):
    experts = jnp.arange(N_EXPERTS, dtype=jnp.int32)
    padded = ((counts + MOE_GROUP_PAD - 1) // MOE_GROUP_PAD) * MOE_GROUP_PAD
    ends = jnp.cumsum(padded)
    offs = ends - padded
    group_start = jnp.sum((top_i[:, :, None] == experts) * offs, axis=-1)
    pos = (group_start + rank).astype(jnp.int32)
    tok = jnp.tile(jnp.arange(TOKENS, dtype=jnp.int32), TOP_K)
    src = jnp.zeros((N_SORT_PAD,), jnp.int32).at[pos.reshape(-1)].set(tok)

    def tile_tables(tm):
        tile_start = jnp.arange(N_SORT_PAD // tm, dtype=jnp.int32) * tm
        te = jnp.sum((tile_start[:, None] >= ends[None, :]).astype(jnp.int32), axis=1)
        te = jnp.minimum(te, N_EXPERTS - 1)
        nv = jnp.clip((offs + counts)[te] - tile_start, 0, tm).astype(jnp.int32)
        nu = (ends[-1] // tm).astype(jnp.int32).reshape(1)
        return te, nv, nu

    return src, tile_tables, pos[0], pos[1]


def _split_w_in(w):
    lat = MLA_Q_RANK + MLA_KV_RANK + MLA_ROPE
    w_lat = jnp.zeros((D_MODEL, TN_IN), BF16).at[:, :lat].set(w[:, :lat].astype(BF16))
    return w_lat, w[:, lat:].astype(BF16)


def _pad_w_q_up(w):
    w = w.reshape(MLA_Q_RANK, MLA_HEADS, MLA_NOPE + MLA_ROPE)
    pad = jnp.zeros((MLA_Q_RANK, MLA_HEADS, MLA_QK_PAD - MLA_NOPE - MLA_ROPE), w.dtype)
    return jnp.concatenate([w, pad], axis=2).reshape(MLA_Q_RANK, MLA_HEADS * MLA_QK_PAD).astype(BF16)


def _perm_w_kv_up(w):
    w = w.reshape(MLA_KV_RANK, MLA_HEADS, MLA_NOPE + MLA_V)
    k = w[:, :, :MLA_NOPE].reshape(MLA_KV_RANK, MLA_HEADS * MLA_NOPE)
    v = w[:, :, MLA_NOPE:].reshape(MLA_KV_RANK, MLA_HEADS * MLA_V)
    return jnp.concatenate([k, v], axis=1).astype(BF16)


def _one_group(tm):
    n_tiles = TOKENS // tm
    return jnp.zeros((n_tiles,), jnp.int32), jnp.full((n_tiles,), tm, jnp.int32), jnp.full((1,), n_tiles, jnp.int32)


def kernel(x, c, positions, ada_w, ada_b, norm_g, w_in, q_norm_g, kv_norm_g, w_q_up, w_kv_up, diff_lambda,
           diff_subln_g, w_branch_a, w_branch_b, b_gate, w_out, dense_w_gu, dense_w_down, router_w, moe_w_gu,
           moe_w_down):
    assert x.shape == (BATCH, SEQ, D_MODEL) and c.shape == (BATCH, D_MODEL)
    assert w_in.shape == (DEPTH, D_MODEL, 8000)
    xt = x.reshape(TOKENS, D_MODEL)
    cos, sin = _rope_tables(positions)
    mod = _adaln(c, ada_w, ada_b)

    for l in range(DEPTH):
        shift, scale, gate = (mod[l, 0, :, :, i * D_MODEL:(i + 1) * D_MODEL] for i in range(3))
        h = _norm_mod(xt, norm_g[l, 0], scale, shift)
        proj = _in_proj(h, *_split_w_in(w_in[l]), cos, sin)
        qa, ka, va = _mla_up(proj, q_norm_g[l], kv_norm_g[l], _pad_w_q_up(w_q_up[l]), _perm_w_kv_up(w_kv_up[l]),
                             cos, sin)
        ya = _mla_attn(qa, ka, va)
        yb = _diff_attn(proj, diff_lambda[l], diff_subln_g[l], l)
        xt = _merge_out(ya, yb, proj, b_gate[l], w_branch_a[l].astype(BF16), w_branch_b[l].astype(BF16),
                        w_out[l].astype(BF16), xt, gate, norm_g[l, 1])

        shift, scale, gate = (mod[l, 1, :, :, i * D_MODEL:(i + 1) * D_MODEL] for i in range(3))
        if l % 2 == 0:
            h = _norm_mod(xt, norm_g[l, 2], scale, shift)
            act = _grouped_gu(h, dense_w_gu[l // 2][None], _one_group(TM_DENSE_GU), TM_DENSE_GU, TN_DENSE_GU)
            y = _grouped_down(act, dense_w_down[l // 2][None], _one_group(TM_DENSE_DOWN), TM_DENSE_DOWN,
                              TN_DENSE_DOWN)
            xt = _post(y, xt, gate, norm_g[l, 3])
        else:
            h, top_i, top_w, rank, cnt = _norm_mod_route(xt, norm_g[l, 2], scale, shift, router_w[l // 2])
            src, tile_tables, pos0, pos1 = _routing_tables(top_i, rank, cnt[:, 0].astype(jnp.int32))
            xs = _dispatch(h, src, tile_tables(TM_DISPATCH)[1])
            act = _grouped_gu(xs, moe_w_gu[l // 2], tile_tables(TM_MOE_GU), TM_MOE_GU, TN_MOE_GU)
            ys = _grouped_down(act, moe_w_down[l // 2], tile_tables(TM_MOE_DOWN), TM_MOE_DOWN, TN_MOE_DOWN)
            xt = _combine(ys, pos0, pos1, top_w.T, xt, gate, norm_g[l, 3])
    return xt.reshape(BATCH, SEQ, D_MODEL)
```

```python
import functools
import math

import jax
import jax.numpy as jnp
import numpy as np
from jax import lax
from jax.experimental import pallas as pl
from jax.experimental.pallas import tpu as pltpu

F32 = jnp.float32
BF16 = jnp.bfloat16

D_MODEL = 2048
BATCH = 2
SEQ = 4096
TOKENS = BATCH * SEQ
DEPTH = 2
CHUNK = 64
ROPE_THETA = 10000.0
EPS = 1e-6
NEG_INF = -1e30

MLA_HEADS = 8
MLA_Q_RANK = 512
MLA_KV_RANK = 256
MLA_NOPE = 128
MLA_ROPE = 64
MLA_V = 128
MLA_QK_PAD = 256

DIFF_HEADS = 8
DIFF_HEAD_DIM = 64

D_FF_DENSE = 5632
N_EXPERTS = 8
TOP_K = 2
D_FF_EXPERT = 7168

D_IN_PAD = 8192

LANES = 128
N_DMA_QUEUES = 2
V7X_VMEM_LIMIT = 56 * 1024 * 1024
V7X_VMEM_LIMIT_GROUPED = 60 * 1024 * 1024

TM_NORM = 512
TM_IN = 2048
TN_IN = 1024
TM_UP = 512
TQ = 1024
TK = 512
TM_MERGE = 256
TM_DENSE_GU = 1024
TM_DENSE_DOWN = 512
TN_DENSE_GU = 512
TN_DENSE_DOWN = 512
MOE_GROUP_PAD = 512
MOE_ROW_QUANTUM = 256
TM_MOE = 512
TM_DISPATCH = 256
TN_MOE_GU = 1024
TN_MOE_DOWN = 512
N_SORT_PAD = TOP_K * TOKENS + N_EXPERTS * MOE_GROUP_PAD
TM_COMBINE = 256


def _cparams(sem, vmem=None):
    return pltpu.CompilerParams(dimension_semantics=sem, vmem_limit_bytes=vmem)


def _nt_dot(a, b):
    return lax.dot_general(a, b, (((1,), (1,)), ((), ())), preferred_element_type=F32)


def _rms(y):
    return y * lax.rsqrt(jnp.mean(y * y, axis=-1, keepdims=True) + EPS)


def _rope_lanes(blk, cos, sin_signed, first_half):
    rot = jnp.where(first_half, pltpu.roll(blk, 96, 1), pltpu.roll(blk, 32, 1))
    return blk * cos + rot * sin_signed


def _first_half_mask(rows):
    lane = lax.broadcasted_iota(jnp.int32, (rows, LANES), 1)
    return (lane % 64) < 32


def _rope_tables_kernel(pos_ref, freq_ref, sign_ref, cos_ref, sin_ref):
    ang = pos_ref[...].astype(F32) * freq_ref[...]
    cos_ref[...] = jnp.cos(ang)
    sin_ref[...] = jnp.sin(ang) * sign_ref[...]


def _rope_tables(positions):
    half = MLA_ROPE // 2
    inv_freq = ROPE_THETA ** (-jnp.arange(0, MLA_ROPE, 2, dtype=F32) / MLA_ROPE)
    freq = jnp.tile(inv_freq, LANES // half).reshape(1, LANES)
    sign = jnp.tile(jnp.concatenate([-jnp.ones((half,), F32), jnp.ones((half,), F32)]), LANES // MLA_ROPE)
    sign = sign.reshape(1, LANES)
    pos = positions.reshape(TOKENS, 1)
    tm = 1024
    return pl.pallas_call(
        _rope_tables_kernel,
        grid=(TOKENS // tm,),
        in_specs=[pl.BlockSpec((tm, 1), lambda i: (i, 0)),
                  pl.BlockSpec((1, LANES), lambda i: (0, 0)),
                  pl.BlockSpec((1, LANES), lambda i: (0, 0))],
        out_specs=[pl.BlockSpec((tm, LANES), lambda i: (i, 0)),
                   pl.BlockSpec((tm, LANES), lambda i: (i, 0))],
        out_shape=[jax.ShapeDtypeStruct((TOKENS, LANES), F32)] * 2,
        compiler_params=_cparams(("parallel",)),
        name="rope_tables",
    )(pos, freq, sign)


def _adaln_kernel(c_ref, w_ref, b_ref, o_ref):
    @pl.when(pl.program_id(1) == 0)
    def _():
        o_ref[0] = jnp.broadcast_to(b_ref[0], o_ref.shape[1:])

    c = c_ref[...]
    cs = (c * jax.nn.sigmoid(c)).astype(BF16)
    o_ref[0] += jnp.dot(cs, w_ref[0].astype(BF16), preferred_element_type=F32)


def _adaln(c, ada_w, ada_b):
    n_mod = DEPTH * 2
    w = ada_w.reshape(n_mod, D_MODEL, 3 * D_MODEL)
    b = ada_b.reshape(n_mod, 1, 3 * D_MODEL)
    c8 = jnp.zeros((8, D_MODEL), F32).at[:BATCH].set(c)
    tk = 256
    out = pl.pallas_call(
        _adaln_kernel,
        grid=(n_mod, D_MODEL // tk),
        in_specs=[pl.BlockSpec((8, tk), lambda m, k: (0, k)),
                  pl.BlockSpec((1, tk, 3 * D_MODEL), lambda m, k: (m, k, 0)),
                  pl.BlockSpec((1, 1, 3 * D_MODEL), lambda m, k: (m, 0, 0))],
        out_specs=pl.BlockSpec((1, 8, 3 * D_MODEL), lambda m, k: (m, 0, 0)),
        out_shape=jax.ShapeDtypeStruct((n_mod, 8, 3 * D_MODEL), F32),
        compiler_params=_cparams(("parallel", "arbitrary"), V7X_VMEM_LIMIT),
        name="adaln",
    )(c8, w, b)
    mod = out[:, :BATCH, :].reshape(DEPTH, 2, BATCH, 1, 3 * D_MODEL)
    return mod


def _norm_mod_kernel(x_ref, g_ref, sc_ref, sh_ref, h_ref):
    h = _rms(x_ref[...]) * g_ref[...] * (1.0 + sc_ref[0]) + sh_ref[0]
    h_ref[...] = h.astype(h_ref.dtype)


def _norm_mod_route_kernel(x_ref, g_ref, sc_ref, sh_ref, rwh_ref, rwl_ref, h_ref, idx_ref, wt_ref, rank_ref,
                           cnt_ref, cnt_s):
    @pl.when(pl.program_id(0) == 0)
    def _():
        cnt_s[...] = jnp.zeros(cnt_s.shape, F32)

    h = _rms(x_ref[...]) * g_ref[...] * (1.0 + sc_ref[0]) + sh_ref[0]
    h_ref[...] = h
    h_hi = h.astype(BF16)
    h_lo = (h - h_hi.astype(F32)).astype(BF16)
    logits = (jnp.dot(h_hi, rwh_ref[...], preferred_element_type=F32)
              + jnp.dot(h_hi, rwl_ref[...], preferred_element_type=F32)
              + jnp.dot(h_lo, rwh_ref[...], preferred_element_type=F32))
    lt = logits.T[:N_EXPERTS, :]
    eid = lax.broadcasted_iota(jnp.int32, lt.shape, 0)
    m1 = jnp.max(lt, axis=0, keepdims=True)
    i1 = jnp.min(jnp.where(lt == m1, eid, N_EXPERTS), axis=0, keepdims=True)
    lt2 = jnp.where(eid == i1, -jnp.inf, lt)
    m2 = jnp.max(lt2, axis=0, keepdims=True)
    i2 = jnp.min(jnp.where(lt2 == m2, eid, N_EXPERTS), axis=0, keepdims=True)
    e2 = jnp.exp(m2 - m1)
    w1 = 1.0 / (1.0 + e2)
    idx_ref[0:1, :] = i1
    idx_ref[1:2, :] = i2
    wt_ref[0:1, :] = w1
    wt_ref[1:2, :] = e2 * w1
    tm = lt.shape[1]
    chosen = jnp.logical_or(eid == i1, eid == i2)
    upper = (lax.broadcasted_iota(jnp.int32, (tm, tm), 0) < lax.broadcasted_iota(jnp.int32, (tm, tm), 1))
    before = jnp.dot(chosen.astype(BF16), upper.astype(BF16), preferred_element_type=F32) + cnt_s[:, 0:1]
    rank_ref[0:1, :] = jnp.sum(jnp.where(eid == i1, before, 0.0), axis=0, keepdims=True).astype(jnp.int32)
    rank_ref[1:2, :] = jnp.sum(jnp.where(eid == i2, before, 0.0), axis=0, keepdims=True).astype(jnp.int32)
    cnt_s[...] = cnt_s[...] + jnp.sum(chosen.astype(F32), axis=1, keepdims=True)
    cnt_ref[...] = cnt_s[...]


def _mod_specs(tm):
    per_b = SEQ // tm
    return [pl.BlockSpec((tm, D_MODEL), lambda i: (i, 0)),
            pl.BlockSpec((1, D_MODEL), lambda i: (0, 0)),
            pl.BlockSpec((1, 1, D_MODEL), lambda i: (i // per_b, 0, 0)),
            pl.BlockSpec((1, 1, D_MODEL), lambda i: (i // per_b, 0, 0))]


def _norm_mod(x, g, scale, shift):
    tm = TM_NORM
    return pl.pallas_call(
        _norm_mod_kernel,
        grid=(TOKENS // tm,),
        in_specs=_mod_specs(tm),
        out_specs=pl.BlockSpec((tm, D_MODEL), lambda i: (i, 0)),
        out_shape=jax.ShapeDtypeStruct((TOKENS, D_MODEL), BF16),
        compiler_params=_cparams(("parallel",)),
        name="norm_mod",
    )(x, g.reshape(1, D_MODEL), scale, shift)


def _norm_mod_route(x, g, scale, shift, router_w):
    tm = TM_NORM
    rw = jnp.zeros((D_MODEL, LANES), F32).at[:, :N_EXPERTS].set(router_w)
    rw_hi = rw.astype(BF16)
    rw_lo = (rw - rw_hi.astype(F32)).astype(BF16)
    return pl.pallas_call(
        _norm_mod_route_kernel,
        grid=(TOKENS // tm,),
        in_specs=_mod_specs(tm) + [pl.BlockSpec((D_MODEL, LANES), lambda i: (0, 0)),
                                   pl.BlockSpec((D_MODEL, LANES), lambda i: (0, 0))],
        out_specs=[pl.BlockSpec((tm, D_MODEL), lambda i: (i, 0)),
                   pl.BlockSpec((TOP_K, tm), lambda i: (0, i)),
                   pl.BlockSpec((TOP_K, tm), lambda i: (0, i)),
                   pl.BlockSpec((TOP_K, tm), lambda i: (0, i)),
                   pl.BlockSpec((N_EXPERTS, LANES), lambda i: (0, 0))],
        out_shape=[jax.ShapeDtypeStruct((TOKENS, D_MODEL), F32),
                   jax.ShapeDtypeStruct((TOP_K, TOKENS), jnp.int32),
                   jax.ShapeDtypeStruct((TOP_K, TOKENS), F32),
                   jax.ShapeDtypeStruct((TOP_K, TOKENS), jnp.int32),
                   jax.ShapeDtypeStruct((N_EXPERTS, LANES), F32)],
        scratch_shapes=[pltpu.VMEM((N_EXPERTS, LANES), F32)],
        compiler_params=_cparams(("arbitrary",)),
        name="norm_mod_route",
    )(x, g.reshape(1, D_MODEL), scale, shift, rw_hi, rw_lo)


def _in_proj_kernel(h_ref, wlat_ref, wrest_ref, cos_ref, sin_ref, o_ref):
    n = pl.program_id(0)
    tm, tn = o_ref.shape

    def project(w_ref):
        return jnp.dot(h_ref[...], w_ref[...], preferred_element_type=F32)

    def rope_cols(acc, lo, hi, mult):
        fh = _first_half_mask(tm)
        cos, sin = cos_ref[...], sin_ref[...]
        for c in range(lo // LANES, hi // LANES):
            blk = acc[:, c * LANES:(c + 1) * LANES]
            o_ref[:, c * LANES:(c + 1) * LANES] = (_rope_lanes(blk, cos, sin, fh) * mult).astype(o_ref.dtype)

    @pl.when(n == 0)
    def _():
        acc = project(wlat_ref)
        lat = MLA_Q_RANK + MLA_KV_RANK
        o_ref[:, :lat] = acc[:, :lat].astype(o_ref.dtype)
        rope_cols(acc, lat, tn, 1.0)

    @pl.when(n == 1)
    def _():
        rope_cols(project(wrest_ref), 0, tn, DIFF_HEAD_DIM ** -0.5)

    @pl.when(n == 2)
    def _():
        rope_cols(project(wrest_ref), 0, tn, 1.0)

    @pl.when(n >= 3)
    def _():
        o_ref[...] = project(wrest_ref).astype(o_ref.dtype)


def _in_proj(h, w_lat, w_rest, cos, sin):
    tm, tn = TM_IN, TN_IN
    return pl.pallas_call(
        _in_proj_kernel,
        grid=(D_IN_PAD // tn, TOKENS // tm),
        in_specs=[pl.BlockSpec((tm, D_MODEL), lambda n, m: (m, 0)),
                  pl.BlockSpec((D_MODEL, tn), lambda n, m: (0, 0)),
                  pl.BlockSpec((D_MODEL, tn), lambda n, m: (0, jnp.maximum(n - 1, 0))),
                  pl.BlockSpec((tm, LANES), lambda n, m: (m, 0)),
                  pl.BlockSpec((tm, LANES), lambda n, m: (m, 0))],
        out_specs=pl.BlockSpec((tm, tn), lambda n, m: (m, n)),
        out_shape=jax.ShapeDtypeStruct((TOKENS, D_IN_PAD), BF16),
        compiler_params=_cparams(("parallel", "parallel"), V7X_VMEM_LIMIT),
        name="in_proj",
    )(h, w_lat, w_rest, cos, sin)


def _mla_up_kernel(lat_ref, qg_ref, kvg_ref, wq_ref, wkv_ref, cos_ref, sin_ref, q_ref, k_ref, v_ref):
    tm = lat_ref.shape[0]
    scale = (MLA_NOPE + MLA_ROPE) ** -0.5
    q_lat = lat_ref[:, :MLA_Q_RANK].astype(F32)
    kv_lat = lat_ref[:, MLA_Q_RANK:MLA_Q_RANK + MLA_KV_RANK].astype(F32)
    k_pe = lat_ref[:, MLA_Q_RANK + MLA_KV_RANK:MLA_Q_RANK + MLA_KV_RANK + LANES]
    qn = (_rms(q_lat) * qg_ref[...]).astype(BF16)
    kvn = (_rms(kv_lat) * kvg_ref[...]).astype(BF16)
    q = jnp.dot(qn, wq_ref[...], preferred_element_type=F32)
    kv = jnp.dot(kvn, wkv_ref[...], preferred_element_type=F32)
    fh = _first_half_mask(tm)
    cos, sin = cos_ref[...], sin_ref[...]
    for h in range(MLA_HEADS):
        c0 = h * MLA_QK_PAD
        q_ref[:, c0:c0 + LANES] = (q[:, c0:c0 + LANES] * scale).astype(BF16)
        q_pe = _rope_lanes(q[:, c0 + LANES:c0 + 2 * LANES], cos, sin, fh)
        q_ref[:, c0 + LANES:c0 + 2 * LANES] = (q_pe * scale).astype(BF16)
        k_ref[:, c0:c0 + LANES] = kv[:, h * LANES:(h + 1) * LANES].astype(BF16)
        k_ref[:, c0 + LANES:c0 + 2 * LANES] = k_pe
    v_ref[...] = kv[:, MLA_HEADS * MLA_NOPE:].astype(BF16)


def _mla_up(proj, q_norm_g, kv_norm_g, wq_p, wkv_p, cos, sin):
    tm = TM_UP
    lat_w = 1024
    hq = MLA_HEADS * MLA_QK_PAD
    hv = MLA_HEADS * MLA_V
    const = lambda i: (0, 0)
    return pl.pallas_call(
        _mla_up_kernel,
        grid=(TOKENS // tm,),
        in_specs=[pl.BlockSpec((tm, lat_w), lambda i: (i, 0)),
                  pl.BlockSpec((1, MLA_Q_RANK), const),
                  pl.BlockSpec((1, MLA_KV_RANK), const),
                  pl.BlockSpec((MLA_Q_RANK, hq), const),
                  pl.BlockSpec((MLA_KV_RANK, 2 * hv), const),
                  pl.BlockSpec((tm, LANES), lambda i: (i, 0)),
                  pl.BlockSpec((tm, LANES), lambda i: (i, 0))],
        out_specs=[pl.BlockSpec((tm, hq), lambda i: (i, 0)),
                   pl.BlockSpec((tm, hq), lambda i: (i, 0)),
                   pl.BlockSpec((tm, hv), lambda i: (i, 0))],
        out_shape=[jax.ShapeDtypeStruct((TOKENS, hq), BF16),
                   jax.ShapeDtypeStruct((TOKENS, hq), BF16),
                   jax.ShapeDtypeStruct((TOKENS, hv), BF16)],
        compiler_params=_cparams(("parallel",), V7X_VMEM_LIMIT),
        name="mla_up",
    )(proj, q_norm_g.reshape(1, -1), kv_norm_g.reshape(1, -1), wq_p, wkv_p, cos, sin)


def _diag_masked(s):
    qc = lax.broadcasted_iota(jnp.int32, s.shape, 0) // CHUNK
    kc = lax.broadcasted_iota(jnp.int32, s.shape, 1) // CHUNK
    return jnp.where(kc <= qc, s, NEG_INF)


def _key_rows(j):
    return pl.ds(pl.multiple_of(j * TK, TK), TK)


def _softmax_pv(s, vaug, m_s, acc_s, row0):
    tk = s.shape[1]
    m_prev = m_s[row0:, :]
    m_new = jnp.maximum(m_prev, jnp.max(s, axis=1, keepdims=True))
    alpha = jnp.exp(m_prev - m_new)
    p = jnp.exp(s - jnp.tile(m_new, (1, tk // LANES)))
    pv = jnp.dot(p.astype(BF16), vaug, preferred_element_type=F32)
    acc_s[row0:, :] = jnp.tile(alpha, (1, 2)) * acc_s[row0:, :] + pv
    m_s[row0:, :] = m_new


def _attn_sweep(qi, scores, vaug_s, states):
    n_diag = TQ // TK
    n_full = qi * n_diag

    def consume(ss, j, row0):
        vaug = vaug_s[_key_rows(j), :]
        for s, (m_s, acc_s) in zip(ss, states):
            _softmax_pv(s, vaug, m_s, acc_s, row0)

    d = n_diag - 1
    pending, idx, row0 = scores(n_full + d, d * TK), n_full + d, d * TK
    for d in reversed(range(n_diag - 1)):
        nxt = scores(n_full + d, d * TK)
        consume(pending, idx, row0)
        pending, idx, row0 = nxt, n_full + d, d * TK

    def body(j, carry):
        nxt = scores(j, None)
        consume(*carry, 0)
        return nxt, j

    consume(*lax.fori_loop(0, n_full, body, (pending, idx)), 0)


def _init_attn_state(v_ref, vaug_s, states):
    @pl.when(pl.program_id(2) == 0)
    def _():
        vaug_s[:, :LANES] = v_ref[...]
        vaug_s[:, LANES:] = jnp.ones((vaug_s.shape[0], LANES), vaug_s.dtype)

    for m_s, acc_s in states:
        m_s[...] = jnp.full(m_s.shape, -jnp.inf, F32)
        acc_s[...] = jnp.zeros(acc_s.shape, F32)


def _attn_scratch(n_states):
    return ([pltpu.VMEM((SEQ, 2 * LANES), BF16)]
            + [pltpu.VMEM((TQ, LANES), F32), pltpu.VMEM((TQ, 2 * LANES), F32)] * n_states)


def _normalised(acc_s):
    return acc_s[:, :LANES] * (1.0 / acc_s[:, LANES:])


def _mla_attn_kernel(q_ref, k_ref, v_ref, o_ref, vaug_s, m_s, acc_s):
    qi = pl.program_id(2)
    states = ((m_s, acc_s),)
    _init_attn_state(v_ref, vaug_s, states)

    def scores(j, row0):
        k = k_ref[_key_rows(j), :]
        if row0 is None:
            return (_nt_dot(q_ref[...], k),)
        return (_diag_masked(_nt_dot(q_ref[row0:, :], k)),)

    _attn_sweep(qi, scores, vaug_s, states)
    o_ref[...] = _normalised(acc_s).astype(o_ref.dtype)


def _mla_attn(qa, ka, va):
    tq = TQ
    nq = SEQ // tq
    return pl.pallas_call(
        _mla_attn_kernel,
        grid=(BATCH, MLA_HEADS, nq),
        in_specs=[pl.BlockSpec((tq, MLA_QK_PAD), lambda b, h, i: (b * nq + i, h)),
                  pl.BlockSpec((SEQ, MLA_QK_PAD), lambda b, h, i: (b, h)),
                  pl.BlockSpec((SEQ, MLA_V), lambda b, h, i: (b, h))],
        out_specs=pl.BlockSpec((tq, MLA_V), lambda b, h, i: (b * nq + i, h)),
        out_shape=jax.ShapeDtypeStruct((TOKENS, MLA_HEADS * MLA_V), BF16),
        scratch_shapes=_attn_scratch(1),
        compiler_params=_cparams(("parallel", "parallel", "arbitrary"), V7X_VMEM_LIMIT),
        name="mla_attn",
    )(qa, ka, va)


def _diff_attn_kernel(q_ref, k_ref, v_ref, lam_ref, g_ref, o_ref, vaug_s, m1, a1, m2, a2, *, lam_init):
    qi = pl.program_id(2)
    states = ((m1, a1), (m2, a2))
    _init_attn_state(v_ref, vaug_s, states)
    q = q_ref[...]
    lane = lax.broadcasted_iota(jnp.int32, q.shape, 1)
    zero = jnp.zeros_like(q)
    q1 = jnp.where(lane < DIFF_HEAD_DIM, q, zero)
    q2 = jnp.where(lane >= DIFF_HEAD_DIM, q, zero)

    def scores(j, row0):
        k = k_ref[_key_rows(j), :]
        if row0 is None:
            return (_nt_dot(q1, k), _nt_dot(q2, k))
        return (_diag_masked(_nt_dot(q1[row0:, :], k)), _diag_masked(_nt_dot(q2[row0:, :], k)))

    _attn_sweep(qi, scores, vaug_s, states)
    lp = lam_ref[...]
    lam = (jnp.exp(jnp.sum(lp[0:1] * lp[1:2], axis=1, keepdims=True))
           - jnp.exp(jnp.sum(lp[2:3] * lp[3:4], axis=1, keepdims=True)) + lam_init)
    od = _normalised(a1) - lam * _normalised(a2)
    o_ref[...] = (_rms(od) * g_ref[...] * (1.0 - lam_init)).astype(o_ref.dtype)


def _diff_attn(proj, diff_lambda, subln_g, layer):
    tq = TQ
    nq = SEQ // tq
    hd = 2 * DIFF_HEAD_DIM
    q0, k0, v0 = 1024 // hd, 2048 // hd, 3072 // hd
    lam_init = 0.8 - 0.6 * math.exp(-0.3 * layer)
    return pl.pallas_call(
        functools.partial(_diff_attn_kernel, lam_init=lam_init),
        grid=(BATCH, DIFF_HEADS, nq),
        in_specs=[pl.BlockSpec((tq, hd), lambda b, h, i: (b * nq + i, q0 + h)),
                  pl.BlockSpec((SEQ, hd), lambda b, h, i: (b, k0 + h)),
                  pl.BlockSpec((SEQ, hd), lambda b, h, i: (b, v0 + h)),
                  pl.BlockSpec((4, DIFF_HEAD_DIM), lambda b, h, i: (0, 0)),
                  pl.BlockSpec((1, hd), lambda b, h, i: (0, 0))],
        out_specs=pl.BlockSpec((tq, hd), lambda b, h, i: (b * nq + i, h)),
        out_shape=jax.ShapeDtypeStruct((TOKENS, DIFF_HEADS * hd), BF16),
        scratch_shapes=_attn_scratch(2),
        compiler_params=_cparams(("parallel", "parallel", "arbitrary"), V7X_VMEM_LIMIT),
        name="diff_attn",
    )(proj, proj, proj, diff_lambda, subln_g.reshape(1, hd))


def _post_residual(x, y, gate, g):
    return x + gate * (_rms(y) * g)


def _merge_out_kernel(ya_ref, yb_ref, ga_ref, gb_ref, bg_ref, wa_ref, wb_ref, wo_ref, x_ref, gate_ref, g_ref,
                      o_ref):
    a = jnp.dot(ya_ref[...], wa_ref[...], preferred_element_type=F32)
    b = jnp.dot(yb_ref[...], wb_ref[...], preferred_element_type=F32)
    bg = bg_ref[...]
    merged = (jax.nn.sigmoid(ga_ref[...].astype(F32) + bg[0:1]) * a
              + jax.nn.sigmoid(gb_ref[...].astype(F32) + bg[1:2]) * b)
    y = jnp.dot(merged.astype(BF16), wo_ref[...], preferred_element_type=F32)
    o_ref[...] = _post_residual(x_ref[...], y, gate_ref[0], g_ref[...])


def _resident(shape):
    return pl.BlockSpec(shape, lambda i: (0,) * len(shape), pipeline_mode=pl.Buffered(1))


def _merge_out(ya, yb, proj, b_gate, wa, wb, wo, x, gate, g):
    tm = TM_MERGE
    per_b = SEQ // tm
    ga_blk, gb_blk = 4096 // D_MODEL, 6144 // D_MODEL
    return pl.pallas_call(
        _merge_out_kernel,
        grid=(TOKENS // tm,),
        in_specs=[pl.BlockSpec((tm, 1024), lambda i: (i, 0)),
                  pl.BlockSpec((tm, 1024), lambda i: (i, 0)),
                  pl.BlockSpec((tm, D_MODEL), lambda i: (i, ga_blk)),
                  pl.BlockSpec((tm, D_MODEL), lambda i: (i, gb_blk)),
                  _resident((2, D_MODEL)),
                  _resident((1024, D_MODEL)),
                  _resident((1024, D_MODEL)),
                  _resident((D_MODEL, D_MODEL)),
                  pl.BlockSpec((tm, D_MODEL), lambda i: (i, 0)),
                  pl.BlockSpec((1, 1, D_MODEL), lambda i: (i // per_b, 0, 0)),
                  _resident((1, D_MODEL))],
        out_specs=pl.BlockSpec((tm, D_MODEL), lambda i: (i, 0)),
        out_shape=jax.ShapeDtypeStruct((TOKENS, D_MODEL), F32),
        compiler_params=_cparams(("parallel",), V7X_VMEM_LIMIT),
        name="merge_out",
    )(ya, yb, proj, proj, b_gate, wa, wb, wo, x, gate, g.reshape(1, D_MODEL))


def _new_weights(te_ref, m):
    return jnp.logical_or(m == 0, te_ref[m] != te_ref[jnp.maximum(m - 1, 0)])


def _grouped_kernel(te_ref, nv_ref, nu_ref, x_ref, *refs, tile_fn):
    n_w = (len(refs) - 1) // 2
    w_refs, o_ref, w_bf16 = refs[:n_w], refs[n_w], refs[n_w + 1:]
    m = pl.program_id(1)
    tm = x_ref.shape[0]
    in_use = m < nu_ref[0]
    n_blocks = jnp.where(in_use, (nv_ref[m] + MOE_ROW_QUANTUM - 1) // MOE_ROW_QUANTUM, 0)

    @pl.when(jnp.logical_and(in_use, _new_weights(te_ref, m)))
    def _():
        for w_ref, w_s in zip(w_refs, w_bf16):
            w_s[...] = w_ref[0].astype(BF16)

    for k in range(1, tm // MOE_ROW_QUANTUM + 1):
        rows = k * MOE_ROW_QUANTUM

        @pl.when(n_blocks == k)
        def _(rows=rows):
            o_ref[:rows, :] = tile_fn(x_ref[:rows, :], *w_bf16).astype(o_ref.dtype)
            if rows < tm:
                o_ref[rows:, :] = jnp.zeros((tm - rows, o_ref.shape[1]), o_ref.dtype)

    @pl.when(n_blocks == 0)
    def _():
        o_ref[...] = jnp.zeros(o_ref.shape, o_ref.dtype)


def _swiglu_tile(x, wg_s, wu_s):
    g = jnp.dot(x, wg_s[...], preferred_element_type=F32)
    u = jnp.dot(x, wu_s[...], preferred_element_type=F32)
    return g * jax.nn.sigmoid(g) * u


def _down_tile(x, w_s):
    return jnp.dot(x, w_s[...], preferred_element_type=F32)


def _grouped_matmul(tile_fn, x, w, tables, tm, tn, n_w, out_dtype, name):
    te, nv, nu = tables
    rows, k = x.shape
    n_tiles = w.shape[-1] // (n_w * tn)
    clamp = lambda m, nu_ref: jnp.minimum(m, nu_ref[0] - 1)
    w_specs = [pl.BlockSpec((1, k, tn),
                            lambda n, m, te_r, nv_r, nu_r, off=i * n_tiles: (te_r[clamp(m, nu_r)], 0, n + off))
               for i in range(n_w)]
    return pl.pallas_call(
        functools.partial(_grouped_kernel, tile_fn=tile_fn),
        grid_spec=pltpu.PrefetchScalarGridSpec(
            num_scalar_prefetch=3,
            grid=(n_tiles, rows // tm),
            in_specs=[pl.BlockSpec((tm, k), lambda n, m, te_r, nv_r, nu_r: (clamp(m, nu_r), 0))] + w_specs,
            out_specs=pl.BlockSpec((tm, tn), lambda n, m, te_r, nv_r, nu_r: (m, n)),
            scratch_shapes=[pltpu.VMEM((k, tn), BF16)] * n_w),
        out_shape=jax.ShapeDtypeStruct((rows, n_tiles * tn), out_dtype),
        compiler_params=_cparams(("arbitrary", "arbitrary"), V7X_VMEM_LIMIT_GROUPED),
        name=name,
    )(te, nv, nu, x, *([w] * n_w))


def _grouped_gu(x, w_gu, tables, tm, tn):
    return _grouped_matmul(_swiglu_tile, x, w_gu, tables, tm, tn, 2, BF16, "grouped_gate_up")


def _grouped_down(x, w_down, tables, tm, tn):
    return _grouped_matmul(_down_tile, x, w_down, tables, tm, tn, 1, F32, "grouped_down")


def _expert_kernel(t0_ref, nt_ref, x_hbm, *refs, tile_fn, tm, tn, n_row_tiles):
    n_w = (len(refs) - 5) // 2
    w_refs, o_hbm, w_bf16 = refs[:n_w], refs[n_w], refs[n_w + 1:2 * n_w + 1]
    xbuf, obuf, sem_x, sem_o = refs[2 * n_w + 1:]
    n, e = pl.program_id(0), pl.program_id(1)
    t0, nt = t0_ref[e], nt_ref[e]
    cols = pl.ds(pl.multiple_of(n * tn, tn), tn)
    row_queue = N_DMA_QUEUES - 1

    def rows(t):
        return pl.ds(pl.multiple_of(t * tm, tm), tm)

    def x_copy(t, slot):
        return pltpu.make_async_copy(x_hbm.at[rows(t0 + t)], xbuf.at[slot], sem_x.at[slot])

    def o_copy(t, slot):
        return pltpu.make_async_copy(obuf.at[slot], o_hbm.at[rows(t), cols], sem_o.at[slot])

    @pl.when(nt > 0)
    def _():
        x_copy(0, 0).start(priority=row_queue)
        for w_ref, w_s in zip(w_refs, w_bf16):
            w_s[...] = w_ref[0].astype(BF16)

        def body(t, carry):
            slot = lax.rem(t, 2)
            x_copy(t, slot).wait()

            @pl.when(t + 1 < nt)
            def _():
                x_copy(t + 1, 1 - slot).start(priority=row_queue)

            @pl.when(t >= 2)
            def _():
                o_copy(t0 + t - 2, slot).wait()

            obuf[slot] = tile_fn(xbuf[slot], *w_bf16).astype(obuf.dtype)
            o_copy(t0 + t, slot).start(priority=row_queue)
            return carry

        lax.fori_loop(0, nt, body, 0)

        @pl.when(nt >= 2)
        def _():
            o_copy(t0 + nt - 2, lax.rem(nt, 2)).wait()

        o_copy(t0 + nt - 1, lax.rem(nt + 1, 2)).wait()

    @pl.when(e == pl.num_programs(1) - 1)
    def _():
        obuf[0] = jnp.zeros(obuf.shape[1:], obuf.dtype)

        def fill(t, carry):
            o_copy(t, 0).start(priority=row_queue)
            o_copy(t, 0).wait()
            return carry

        lax.fori_loop(t0 + nt, n_row_tiles, fill, 0)


def _expert_matmul(tile_fn, x, w, t0, nt, tm, tn, n_w, out_dtype, name):
    rows, k = x.shape
    n_tiles = w.shape[-1] // (n_w * tn)
    w_specs = [pl.BlockSpec((1, k, tn), lambda n, e, a, b, off=i * n_tiles: (e, 0, n + off)) for i in range(n_w)]
    return pl.pallas_call(
        functools.partial(_expert_kernel, tile_fn=tile_fn, tm=tm, tn=tn, n_row_tiles=rows // tm),
        grid_spec=pltpu.PrefetchScalarGridSpec(
            num_scalar_prefetch=2,
            grid=(n_tiles, w.shape[0]),
            in_specs=[pl.BlockSpec(memory_space=pl.ANY)] + w_specs,
            out_specs=pl.BlockSpec(memory_space=pl.ANY),
            scratch_shapes=([pltpu.VMEM((k, tn), BF16)] * n_w
                            + [pltpu.VMEM((2, tm, k), x.dtype), pltpu.VMEM((2, tm, tn), out_dtype),
                               pltpu.SemaphoreType.DMA((2,)), pltpu.SemaphoreType.DMA((2,))])),
        out_shape=jax.ShapeDtypeStruct((rows, n_tiles * tn), out_dtype),
        compiler_params=_cparams(("arbitrary", "arbitrary"), V7X_VMEM_LIMIT_GROUPED),
        name=name,
    )(t0, nt, x, *([w] * n_w))


def _post_kernel(y_ref, x_ref, gate_ref, g_ref, o_ref):
    o_ref[...] = _post_residual(x_ref[...], y_ref[...], gate_ref[0], g_ref[...])


def _post(y, x, gate, g):
    tm = TM_NORM
    per_b = SEQ // tm
    return pl.pallas_call(
        _post_kernel,
        grid=(TOKENS // tm,),
        in_specs=[pl.BlockSpec((tm, D_MODEL), lambda i: (i, 0)),
                  pl.BlockSpec((tm, D_MODEL), lambda i: (i, 0)),
                  pl.BlockSpec((1, 1, D_MODEL), lambda i: (i // per_b, 0, 0)),
                  pl.BlockSpec((1, D_MODEL), lambda i: (0, 0))],
        out_specs=pl.BlockSpec((tm, D_MODEL), lambda i: (i, 0)),
        out_shape=jax.ShapeDtypeStruct((TOKENS, D_MODEL), F32),
        compiler_params=_cparams(("parallel",), V7X_VMEM_LIMIT),
        name="post_norm_residual",
    )(y, x, gate, g.reshape(1, D_MODEL))


def _row_copy(src_hbm, row, dst, slot, sem):
    return pltpu.make_async_copy(src_hbm.at[pl.ds(row, 1)], dst.at[pl.ds(slot, 1)], sem)


def _dispatch_kernel(src_ref, nv_ref, h_hbm, o_ref, buf, sem):
    i = pl.program_id(0)
    tm = buf.shape[0]

    @pl.when(nv_ref[i] > 0)
    def _():
        base = i * tm

        def issue(pair, c):
            for q in range(N_DMA_QUEUES):
                r = pair * N_DMA_QUEUES + q
                _row_copy(h_hbm, src_ref[base + r], buf, r, sem).start(priority=q)
            return c

        def wait(r, c):
            _row_copy(h_hbm, 0, buf, r, sem).wait()
            return c

        lax.fori_loop(0, tm // N_DMA_QUEUES, issue, 0)
        lax.fori_loop(0, tm, wait, 0)
        o_ref[...] = buf[...].astype(o_ref.dtype)

    @pl.when(nv_ref[i] == 0)
    def _():
        o_ref[...] = jnp.zeros(o_ref.shape, o_ref.dtype)


def _dispatch(h, src, nv):
    tm = TM_DISPATCH
    return pl.pallas_call(
        _dispatch_kernel,
        grid_spec=pltpu.PrefetchScalarGridSpec(
            num_scalar_prefetch=2,
            grid=(N_SORT_PAD // tm,),
            in_specs=[pl.BlockSpec(memory_space=pl.ANY)],
            out_specs=pl.BlockSpec((tm, D_MODEL), lambda i, s_r, nv_r: (i, 0)),
            scratch_shapes=[pltpu.VMEM((tm, D_MODEL), F32), pltpu.SemaphoreType.DMA(())]),
        out_shape=jax.ShapeDtypeStruct((N_SORT_PAD, D_MODEL), BF16),
        compiler_params=_cparams(("arbitrary",)),
        name="moe_dispatch",
    )(src, nv, h)


def _combine_kernel(p0_ref, p1_ref, ys_hbm, wt_ref, x_ref, gate_ref, g_ref, o_ref, buf0, buf1, sem):
    i = pl.program_id(0)
    tm = buf0.shape[0]
    base = i * tm

    def issue(r, c):
        _row_copy(ys_hbm, p0_ref[base + r], buf0, r, sem).start(priority=0)
        _row_copy(ys_hbm, p1_ref[base + r], buf1, r, sem).start(priority=1)
        return c

    def wait(r, c):
        _row_copy(ys_hbm, 0, buf0, r, sem).wait()
        _row_copy(ys_hbm, 0, buf1, r, sem).wait()
        return c

    lax.fori_loop(0, tm, issue, 0)
    lax.fori_loop(0, tm, wait, 0)
    wt = wt_ref[...]
    y = wt[:, 0:1] * buf0[...] + wt[:, 1:2] * buf1[...]
    o_ref[...] = _post_residual(x_ref[...], y, gate_ref[0], g_ref[...])


def _combine(ys, pos0, pos1, wt, x, gate, g):
    tm = TM_COMBINE
    per_b = SEQ // tm
    return pl.pallas_call(
        _combine_kernel,
        grid_spec=pltpu.PrefetchScalarGridSpec(
            num_scalar_prefetch=2,
            grid=(TOKENS // tm,),
            in_specs=[pl.BlockSpec(memory_space=pl.ANY),
                      pl.BlockSpec((tm, TOP_K), lambda i, a, b: (i, 0)),
                      pl.BlockSpec((tm, D_MODEL), lambda i, a, b: (i, 0)),
                      pl.BlockSpec((1, 1, D_MODEL), lambda i, a, b: (i // per_b, 0, 0)),
                      pl.BlockSpec((1, D_MODEL), lambda i, a, b: (0, 0))],
            out_specs=pl.BlockSpec((tm, D_MODEL), lambda i, a, b: (i, 0)),
            scratch_shapes=[pltpu.VMEM((tm, D_MODEL), F32), pltpu.VMEM((tm, D_MODEL), F32),
                            pltpu.SemaphoreType.DMA(())]),
        out_shape=jax.ShapeDtypeStruct((TOKENS, D_MODEL), F32),
        compiler_params=_cparams(("arbitrary",)),
        name="moe_combine",
    )(pos0, pos1, ys, wt, x, gate, g.reshape(1, D_MODEL))


def _routing_tables(top_i, rank, counts):
    experts = jnp.arange(N_EXPERTS, dtype=jnp.int32)
    padded = ((counts + MOE_GROUP_PAD - 1) // MOE_GROUP_PAD) * MOE_GROUP_PAD
    ends = jnp.cumsum(padded)
    offs = ends - padded
    group_start = jnp.sum((top_i[:, :, None] == experts) * offs, axis=-1)
    pos = (group_start + rank).astype(jnp.int32)
    tok = jnp.tile(jnp.arange(TOKENS, dtype=jnp.int32), TOP_K)
    src = jnp.zeros((N_SORT_PAD,), jnp.int32).at[pos.reshape(-1)].set(tok)

    def tile_tables(tm):
        tile_start = jnp.arange(N_SORT_PAD // tm, dtype=jnp.int32) * tm
        te = jnp.sum((tile_start[:, None] >= ends[None, :]).astype(jnp.int32), axis=1)
        te = jnp.minimum(te, N_EXPERTS - 1)
        nv = jnp.clip((offs + counts)[te] - tile_start, 0, tm).astype(jnp.int32)
        nu = (ends[-1] // tm).astype(jnp.int32).reshape(1)
        return te, nv, nu

    group_tiles = ((offs // TM_MOE).astype(jnp.int32), (padded // TM_MOE).astype(jnp.int32))
    return src, tile_tables, group_tiles, pos[0], pos[1]


def _split_w_in(w):
    lat = MLA_Q_RANK + MLA_KV_RANK + MLA_ROPE
    w_lat = jnp.zeros((D_MODEL, TN_IN), BF16).at[:, :lat].set(w[:, :lat].astype(BF16))
    return w_lat, w[:, lat:].astype(BF16)


def _pad_w_q_up(w):
    w = w.reshape(MLA_Q_RANK, MLA_HEADS, MLA_NOPE + MLA_ROPE)
    pad = jnp.zeros((MLA_Q_RANK, MLA_HEADS, MLA_QK_PAD - MLA_NOPE - MLA_ROPE), w.dtype)
    return jnp.concatenate([w, pad], axis=2).reshape(MLA_Q_RANK, MLA_HEADS * MLA_QK_PAD).astype(BF16)


def _perm_w_kv_up(w):
    w = w.reshape(MLA_KV_RANK, MLA_HEADS, MLA_NOPE + MLA_V)
    k = w[:, :, :MLA_NOPE].reshape(MLA_KV_RANK, MLA_HEADS * MLA_NOPE)
    v = w[:, :, MLA_NOPE:].reshape(MLA_KV_RANK, MLA_HEADS * MLA_V)
    return jnp.concatenate([k, v], axis=1).astype(BF16)


def _one_group(tm):
    n_tiles = TOKENS // tm
    return jnp.zeros((n_tiles,), jnp.int32), jnp.full((n_tiles,), tm, jnp.int32), jnp.full((1,), n_tiles, jnp.int32)


def kernel(x, c, positions, ada_w, ada_b, norm_g, w_in, q_norm_g, kv_norm_g, w_q_up, w_kv_up, diff_lambda,
           diff_subln_g, w_branch_a, w_branch_b, b_gate, w_out, dense_w_gu, dense_w_down, router_w, moe_w_gu,
           moe_w_down):
    assert x.shape == (BATCH, SEQ, D_MODEL) and c.shape == (BATCH, D_MODEL)
    assert w_in.shape == (DEPTH, D_MODEL, 8000)
    xt = x.reshape(TOKENS, D_MODEL)
    cos, sin = _rope_tables(positions)
    mod = _adaln(c, ada_w, ada_b)

    for l in range(DEPTH):
        shift, scale, gate = (mod[l, 0, :, :, i * D_MODEL:(i + 1) * D_MODEL] for i in range(3))
        h = _norm_mod(xt, norm_g[l, 0], scale, shift)
        proj = _in_proj(h, *_split_w_in(w_in[l]), cos, sin)
        qa, ka, va = _mla_up(proj, q_norm_g[l], kv_norm_g[l], _pad_w_q_up(w_q_up[l]), _perm_w_kv_up(w_kv_up[l]),
                             cos, sin)
        ya = _mla_attn(qa, ka, va)
        yb = _diff_attn(proj, diff_lambda[l], diff_subln_g[l], l)
        xt = _merge_out(ya, yb, proj, b_gate[l], w_branch_a[l].astype(BF16), w_branch_b[l].astype(BF16),
                        w_out[l].astype(BF16), xt, gate, norm_g[l, 1])

        shift, scale, gate = (mod[l, 1, :, :, i * D_MODEL:(i + 1) * D_MODEL] for i in range(3))
        if l % 2 == 0:
            h = _norm_mod(xt, norm_g[l, 2], scale, shift)
            act = _grouped_gu(h, dense_w_gu[l // 2][None], _one_group(TM_DENSE_GU), TM_DENSE_GU, TN_DENSE_GU)
            y = _grouped_down(act, dense_w_down[l // 2][None], _one_group(TM_DENSE_DOWN), TM_DENSE_DOWN,
                              TN_DENSE_DOWN)
            xt = _post(y, xt, gate, norm_g[l, 3])
        else:
            h, top_i, top_w, rank, cnt = _norm_mod_route(xt, norm_g[l, 2], scale, shift, router_w[l // 2])
            src, tile_tables, (t0, nt), pos0, pos1 = _routing_tables(top_i, rank, cnt[:, 0].astype(jnp.int32))
            xs = _dispatch(h, src, tile_tables(TM_DISPATCH)[1])
            act = _expert_matmul(_swiglu_tile, xs, moe_w_gu[l // 2], t0, nt, TM_MOE, TN_MOE_GU, 2, BF16,
                                 "expert_gate_up")
            ys = _expert_matmul(_down_tile, act, moe_w_down[l // 2], t0, nt, TM_MOE, TN_MOE_DOWN, 1, F32,
                                "expert_down")
            xt = _combine(ys, pos0, pos1, top_w.T, xt, gate, norm_g[l, 3])
    return xt.reshape(BATCH, SEQ, D_MODEL)
```

```python
import functools
import math

import jax
import jax.numpy as jnp
import numpy as np
from jax import lax
from jax.experimental import pallas as pl
from jax.experimental.pallas import tpu as pltpu

F32 = jnp.float32
BF16 = jnp.bfloat16

D_MODEL = 2048
BATCH = 2
SEQ = 4096
TOKENS = BATCH * SEQ
DEPTH = 2
CHUNK = 64
ROPE_THETA = 10000.0
EPS = 1e-6
NEG_INF = -1e30

MLA_HEADS = 8
MLA_Q_RANK = 512
MLA_KV_RANK = 256
MLA_NOPE = 128
MLA_ROPE = 64
MLA_V = 128
MLA_QK_PAD = 256

DIFF_HEADS = 8
DIFF_HEAD_DIM = 64

D_FF_DENSE = 5632
N_EXPERTS = 8
TOP_K = 2
D_FF_EXPERT = 7168

D_IN_PAD = 8192

LANES = 128
N_DMA_QUEUES = 2
V7X_VMEM_LIMIT = 56 * 1024 * 1024
V7X_VMEM_LIMIT_GROUPED = 60 * 1024 * 1024

TM_NORM = 512
TM_IN = 2048
TN_IN = 1024
TM_UP = 512
TQ = 1024
TK = 512
TM_MERGE = 256
TM_DENSE_GU = 1024
TM_DENSE_DOWN = 512
TN_DENSE_GU = 512
TN_DENSE_DOWN = 512
TM_MOE = 512
TM_DISPATCH = 256
TN_MOE_GU = 1024
TN_MOE_DOWN = 512
N_SORT_PAD = TOP_K * TOKENS + N_EXPERTS * TM_MOE
TM_COMBINE = 256


def _cparams(sem, vmem=None):
    return pltpu.CompilerParams(dimension_semantics=sem, vmem_limit_bytes=vmem)


def _nt_dot(a, b):
    return lax.dot_general(a, b, (((1,), (1,)), ((), ())), preferred_element_type=F32)


def _rms(y):
    return y * lax.rsqrt(jnp.mean(y * y, axis=-1, keepdims=True) + EPS)


def _rope_lanes(blk, cos, sin_signed, first_half):
    rot = jnp.where(first_half, pltpu.roll(blk, 96, 1), pltpu.roll(blk, 32, 1))
    return blk * cos + rot * sin_signed


def _first_half_mask(rows):
    lane = lax.broadcasted_iota(jnp.int32, (rows, LANES), 1)
    return (lane % 64) < 32


def _rope_tables_kernel(pos_ref, freq_ref, sign_ref, cos_ref, sin_ref):
    ang = pos_ref[...].astype(F32) * freq_ref[...]
    cos_ref[...] = jnp.cos(ang)
    sin_ref[...] = jnp.sin(ang) * sign_ref[...]


def _rope_tables(positions):
    half = MLA_ROPE // 2
    inv_freq = ROPE_THETA ** (-jnp.arange(0, MLA_ROPE, 2, dtype=F32) / MLA_ROPE)
    freq = jnp.tile(inv_freq, LANES // half).reshape(1, LANES)
    sign = jnp.tile(jnp.concatenate([-jnp.ones((half,), F32), jnp.ones((half,), F32)]), LANES // MLA_ROPE)
    sign = sign.reshape(1, LANES)
    pos = positions.reshape(TOKENS, 1)
    tm = 1024
    return pl.pallas_call(
        _rope_tables_kernel,
        grid=(TOKENS // tm,),
        in_specs=[pl.BlockSpec((tm, 1), lambda i: (i, 0)),
                  pl.BlockSpec((1, LANES), lambda i: (0, 0)),
                  pl.BlockSpec((1, LANES), lambda i: (0, 0))],
        out_specs=[pl.BlockSpec((tm, LANES), lambda i: (i, 0)),
                   pl.BlockSpec((tm, LANES), lambda i: (i, 0))],
        out_shape=[jax.ShapeDtypeStruct((TOKENS, LANES), F32)] * 2,
        compiler_params=_cparams(("parallel",)),
        name="rope_tables",
    )(pos, freq, sign)


def _adaln_kernel(c_ref, w_ref, b_ref, o_ref):
    @pl.when(pl.program_id(1) == 0)
    def _():
        o_ref[0] = jnp.broadcast_to(b_ref[0], o_ref.shape[1:])

    c = c_ref[...]
    cs = (c * jax.nn.sigmoid(c)).astype(BF16)
    o_ref[0] += jnp.dot(cs, w_ref[0].astype(BF16), preferred_element_type=F32)


def _adaln(c, ada_w, ada_b):
    n_mod = DEPTH * 2
    w = ada_w.reshape(n_mod, D_MODEL, 3 * D_MODEL)
    b = ada_b.reshape(n_mod, 1, 3 * D_MODEL)
    c8 = jnp.zeros((8, D_MODEL), F32).at[:BATCH].set(c)
    tk = 256
    out = pl.pallas_call(
        _adaln_kernel,
        grid=(n_mod, D_MODEL // tk),
        in_specs=[pl.BlockSpec((8, tk), lambda m, k: (0, k)),
                  pl.BlockSpec((1, tk, 3 * D_MODEL), lambda m, k: (m, k, 0)),
                  pl.BlockSpec((1, 1, 3 * D_MODEL), lambda m, k: (m, 0, 0))],
        out_specs=pl.BlockSpec((1, 8, 3 * D_MODEL), lambda m, k: (m, 0, 0)),
        out_shape=jax.ShapeDtypeStruct((n_mod, 8, 3 * D_MODEL), F32),
        compiler_params=_cparams(("parallel", "arbitrary"), V7X_VMEM_LIMIT),
        name="adaln",
    )(c8, w, b)
    mod = out[:, :BATCH, :].reshape(DEPTH, 2, BATCH, 1, 3 * D_MODEL)
    return mod


def _norm_mod_kernel(x_ref, g_ref, sc_ref, sh_ref, h_ref):
    h = _rms(x_ref[...]) * g_ref[...] * (1.0 + sc_ref[0]) + sh_ref[0]
    h_ref[...] = h.astype(h_ref.dtype)


def _norm_mod_route_kernel(x_ref, g_ref, sc_ref, sh_ref, rwh_ref, rwl_ref, h_ref, idx_ref, wt_ref, rank_ref,
                           cnt_ref, cnt_s):
    @pl.when(pl.program_id(0) == 0)
    def _():
        cnt_s[...] = jnp.zeros(cnt_s.shape, F32)

    h = _rms(x_ref[...]) * g_ref[...] * (1.0 + sc_ref[0]) + sh_ref[0]
    h_ref[...] = h
    h_hi = h.astype(BF16)
    h_lo = (h - h_hi.astype(F32)).astype(BF16)
    logits = (jnp.dot(h_hi, rwh_ref[...], preferred_element_type=F32)
              + jnp.dot(h_hi, rwl_ref[...], preferred_element_type=F32)
              + jnp.dot(h_lo, rwh_ref[...], preferred_element_type=F32))
    lt = logits.T[:N_EXPERTS, :]
    eid = lax.broadcasted_iota(jnp.int32, lt.shape, 0)
    m1 = jnp.max(lt, axis=0, keepdims=True)
    i1 = jnp.min(jnp.where(lt == m1, eid, N_EXPERTS), axis=0, keepdims=True)
    lt2 = jnp.where(eid == i1, -jnp.inf, lt)
    m2 = jnp.max(lt2, axis=0, keepdims=True)
    i2 = jnp.min(jnp.where(lt2 == m2, eid, N_EXPERTS), axis=0, keepdims=True)
    e2 = jnp.exp(m2 - m1)
    w1 = 1.0 / (1.0 + e2)
    idx_ref[0:1, :] = i1
    idx_ref[1:2, :] = i2
    wt_ref[0:1, :] = w1
    wt_ref[1:2, :] = e2 * w1
    tm = lt.shape[1]
    chosen = jnp.logical_or(eid == i1, eid == i2)
    upper = (lax.broadcasted_iota(jnp.int32, (tm, tm), 0) < lax.broadcasted_iota(jnp.int32, (tm, tm), 1))
    before = jnp.dot(chosen.astype(BF16), upper.astype(BF16), preferred_element_type=F32) + cnt_s[:, 0:1]
    rank_ref[0:1, :] = jnp.sum(jnp.where(eid == i1, before, 0.0), axis=0, keepdims=True).astype(jnp.int32)
    rank_ref[1:2, :] = jnp.sum(jnp.where(eid == i2, before, 0.0), axis=0, keepdims=True).astype(jnp.int32)
    cnt_s[...] = cnt_s[...] + jnp.sum(chosen.astype(F32), axis=1, keepdims=True)
    cnt_ref[...] = cnt_s[...]


def _mod_specs(tm):
    per_b = SEQ // tm
    return [pl.BlockSpec((tm, D_MODEL), lambda i: (i, 0)),
            pl.BlockSpec((1, D_MODEL), lambda i: (0, 0)),
            pl.BlockSpec((1, 1, D_MODEL), lambda i: (i // per_b, 0, 0)),
            pl.BlockSpec((1, 1, D_MODEL), lambda i: (i // per_b, 0, 0))]


def _norm_mod(x, g, scale, shift):
    tm = TM_NORM
    return pl.pallas_call(
        _norm_mod_kernel,
        grid=(TOKENS // tm,),
        in_specs=_mod_specs(tm),
        out_specs=pl.BlockSpec((tm, D_MODEL), lambda i: (i, 0)),
        out_shape=jax.ShapeDtypeStruct((TOKENS, D_MODEL), BF16),
        compiler_params=_cparams(("parallel",)),
        name="norm_mod",
    )(x, g.reshape(1, D_MODEL), scale, shift)


def _norm_mod_route(x, g, scale, shift, router_w):
    tm = TM_NORM
    rw = jnp.zeros((D_MODEL, LANES), F32).at[:, :N_EXPERTS].set(router_w)
    rw_hi = rw.astype(BF16)
    rw_lo = (rw - rw_hi.astype(F32)).astype(BF16)
    return pl.pallas_call(
        _norm_mod_route_kernel,
        grid=(TOKENS // tm,),
        in_specs=_mod_specs(tm) + [pl.BlockSpec((D_MODEL, LANES), lambda i: (0, 0)),
                                   pl.BlockSpec((D_MODEL, LANES), lambda i: (0, 0))],
        out_specs=[pl.BlockSpec((tm, D_MODEL), lambda i: (i, 0)),
                   pl.BlockSpec((TOP_K, tm), lambda i: (0, i)),
                   pl.BlockSpec((TOP_K, tm), lambda i: (0, i)),
                   pl.BlockSpec((TOP_K, tm), lambda i: (0, i)),
                   pl.BlockSpec((N_EXPERTS, LANES), lambda i: (0, 0))],
        out_shape=[jax.ShapeDtypeStruct((TOKENS, D_MODEL), F32),
                   jax.ShapeDtypeStruct((TOP_K, TOKENS), jnp.int32),
                   jax.ShapeDtypeStruct((TOP_K, TOKENS), F32),
                   jax.ShapeDtypeStruct((TOP_K, TOKENS), jnp.int32),
                   jax.ShapeDtypeStruct((N_EXPERTS, LANES), F32)],
        scratch_shapes=[pltpu.VMEM((N_EXPERTS, LANES), F32)],
        compiler_params=_cparams(("arbitrary",)),
        name="norm_mod_route",
    )(x, g.reshape(1, D_MODEL), scale, shift, rw_hi, rw_lo)


def _in_proj_kernel(h_ref, wlat_ref, wrest_ref, cos_ref, sin_ref, o_ref):
    n = pl.program_id(0)
    tm, tn = o_ref.shape

    def project(w_ref):
        return jnp.dot(h_ref[...], w_ref[...], preferred_element_type=F32)

    def rope_cols(acc, lo, hi, mult):
        fh = _first_half_mask(tm)
        cos, sin = cos_ref[...], sin_ref[...]
        for c in range(lo // LANES, hi // LANES):
            blk = acc[:, c * LANES:(c + 1) * LANES]
            o_ref[:, c * LANES:(c + 1) * LANES] = (_rope_lanes(blk, cos, sin, fh) * mult).astype(o_ref.dtype)

    @pl.when(n == 0)
    def _():
        acc = project(wlat_ref)
        lat = MLA_Q_RANK + MLA_KV_RANK
        o_ref[:, :lat] = acc[:, :lat].astype(o_ref.dtype)
        rope_cols(acc, lat, tn, 1.0)

    @pl.when(n == 1)
    def _():
        rope_cols(project(wrest_ref), 0, tn, DIFF_HEAD_DIM ** -0.5)

    @pl.when(n == 2)
    def _():
        rope_cols(project(wrest_ref), 0, tn, 1.0)

    @pl.when(n >= 3)
    def _():
        o_ref[...] = project(wrest_ref).astype(o_ref.dtype)


def _in_proj(h, w_lat, w_rest, cos, sin):
    tm, tn = TM_IN, TN_IN
    return pl.pallas_call(
        _in_proj_kernel,
        grid=(D_IN_PAD // tn, TOKENS // tm),
        in_specs=[pl.BlockSpec((tm, D_MODEL), lambda n, m: (m, 0)),
                  pl.BlockSpec((D_MODEL, tn), lambda n, m: (0, 0)),
                  pl.BlockSpec((D_MODEL, tn), lambda n, m: (0, jnp.maximum(n - 1, 0))),
                  pl.BlockSpec((tm, LANES), lambda n, m: (m, 0)),
                  pl.BlockSpec((tm, LANES), lambda n, m: (m, 0))],
        out_specs=pl.BlockSpec((tm, tn), lambda n, m: (m, n)),
        out_shape=jax.ShapeDtypeStruct((TOKENS, D_IN_PAD), BF16),
        compiler_params=_cparams(("parallel", "parallel"), V7X_VMEM_LIMIT),
        name="in_proj",
    )(h, w_lat, w_rest, cos, sin)


def _mla_up_kernel(lat_ref, qg_ref, kvg_ref, wq_ref, wkv_ref, cos_ref, sin_ref, q_ref, k_ref, v_ref):
    tm = lat_ref.shape[0]
    scale = (MLA_NOPE + MLA_ROPE) ** -0.5
    q_lat = lat_ref[:, :MLA_Q_RANK].astype(F32)
    kv_lat = lat_ref[:, MLA_Q_RANK:MLA_Q_RANK + MLA_KV_RANK].astype(F32)
    k_pe = lat_ref[:, MLA_Q_RANK + MLA_KV_RANK:MLA_Q_RANK + MLA_KV_RANK + LANES]
    qn = (_rms(q_lat) * qg_ref[...]).astype(BF16)
    kvn = (_rms(kv_lat) * kvg_ref[...]).astype(BF16)
    q = jnp.dot(qn, wq_ref[...], preferred_element_type=F32)
    kv = jnp.dot(kvn, wkv_ref[...], preferred_element_type=F32)
    fh = _first_half_mask(tm)
    cos, sin = cos_ref[...], sin_ref[...]
    for h in range(MLA_HEADS):
        c0 = h * MLA_QK_PAD
        q_ref[:, c0:c0 + LANES] = (q[:, c0:c0 + LANES] * scale).astype(BF16)
        q_pe = _rope_lanes(q[:, c0 + LANES:c0 + 2 * LANES], cos, sin, fh)
        q_ref[:, c0 + LANES:c0 + 2 * LANES] = (q_pe * scale).astype(BF16)
        k_ref[:, c0:c0 + LANES] = kv[:, h * LANES:(h + 1) * LANES].astype(BF16)
        k_ref[:, c0 + LANES:c0 + 2 * LANES] = k_pe
    v_ref[...] = kv[:, MLA_HEADS * MLA_NOPE:].astype(BF16)


def _mla_up(proj, q_norm_g, kv_norm_g, wq_p, wkv_p, cos, sin):
    tm = TM_UP
    lat_w = 1024
    hq = MLA_HEADS * MLA_QK_PAD
    hv = MLA_HEADS * MLA_V
    const = lambda i: (0, 0)
    return pl.pallas_call(
        _mla_up_kernel,
        grid=(TOKENS // tm,),
        in_specs=[pl.BlockSpec((tm, lat_w), lambda i: (i, 0)),
                  pl.BlockSpec((1, MLA_Q_RANK), const),
                  pl.BlockSpec((1, MLA_KV_RANK), const),
                  pl.BlockSpec((MLA_Q_RANK, hq), const),
                  pl.BlockSpec((MLA_KV_RANK, 2 * hv), const),
                  pl.BlockSpec((tm, LANES), lambda i: (i, 0)),
                  pl.BlockSpec((tm, LANES), lambda i: (i, 0))],
        out_specs=[pl.BlockSpec((tm, hq), lambda i: (i, 0)),
                   pl.BlockSpec((tm, hq), lambda i: (i, 0)),
                   pl.BlockSpec((tm, hv), lambda i: (i, 0))],
        out_shape=[jax.ShapeDtypeStruct((TOKENS, hq), BF16),
                   jax.ShapeDtypeStruct((TOKENS, hq), BF16),
                   jax.ShapeDtypeStruct((TOKENS, hv), BF16)],
        compiler_params=_cparams(("parallel",), V7X_VMEM_LIMIT),
        name="mla_up",
    )(proj, q_norm_g.reshape(1, -1), kv_norm_g.reshape(1, -1), wq_p, wkv_p, cos, sin)


def _diag_masked(s):
    qc = lax.broadcasted_iota(jnp.int32, s.shape, 0) // CHUNK
    kc = lax.broadcasted_iota(jnp.int32, s.shape, 1) // CHUNK
    return jnp.where(kc <= qc, s, NEG_INF)


def _key_rows(j):
    return pl.ds(pl.multiple_of(j * TK, TK), TK)


def _softmax_pv(s, vaug, m_s, acc_s, row0):
    tk = s.shape[1]
    m_prev = m_s[row0:, :]
    m_new = jnp.maximum(m_prev, jnp.max(s, axis=1, keepdims=True))
    alpha = jnp.exp(m_prev - m_new)
    p = jnp.exp(s - jnp.tile(m_new, (1, tk // LANES)))
    pv = jnp.dot(p.astype(BF16), vaug, preferred_element_type=F32)
    acc_s[row0:, :] = jnp.tile(alpha, (1, 2)) * acc_s[row0:, :] + pv
    m_s[row0:, :] = m_new


def _attn_sweep(qi, scores, vaug_s, states):
    n_diag = TQ // TK
    n_full = qi * n_diag

    def consume(ss, j, row0):
        vaug = vaug_s[_key_rows(j), :]
        for s, (m_s, acc_s) in zip(ss, states):
            _softmax_pv(s, vaug, m_s, acc_s, row0)

    d = n_diag - 1
    pending, idx, row0 = scores(n_full + d, d * TK), n_full + d, d * TK
    for d in reversed(range(n_diag - 1)):
        nxt = scores(n_full + d, d * TK)
        consume(pending, idx, row0)
        pending, idx, row0 = nxt, n_full + d, d * TK

    def body(j, carry):
        nxt = scores(j, None)
        consume(*carry, 0)
        return nxt, j

    consume(*lax.fori_loop(0, n_full, body, (pending, idx)), 0)


def _init_attn_state(v_ref, vaug_s, states):
    @pl.when(pl.program_id(2) == 0)
    def _():
        vaug_s[:, :LANES] = v_ref[...]
        vaug_s[:, LANES:] = jnp.ones((vaug_s.shape[0], LANES), vaug_s.dtype)

    for m_s, acc_s in states:
        m_s[...] = jnp.full(m_s.shape, -jnp.inf, F32)
        acc_s[...] = jnp.zeros(acc_s.shape, F32)


def _attn_scratch(n_states):
    return ([pltpu.VMEM((SEQ, 2 * LANES), BF16)]
            + [pltpu.VMEM((TQ, LANES), F32), pltpu.VMEM((TQ, 2 * LANES), F32)] * n_states)


def _normalised(acc_s):
    return acc_s[:, :LANES] * (1.0 / acc_s[:, LANES:])


def _mla_attn_kernel(q_ref, k_ref, v_ref, o_ref, vaug_s, m_s, acc_s):
    qi = pl.program_id(2)
    states = ((m_s, acc_s),)
    _init_attn_state(v_ref, vaug_s, states)

    def scores(j, row0):
        k = k_ref[_key_rows(j), :]
        if row0 is None:
            return (_nt_dot(q_ref[...], k),)
        return (_diag_masked(_nt_dot(q_ref[row0:, :], k)),)

    _attn_sweep(qi, scores, vaug_s, states)
    o_ref[...] = _normalised(acc_s).astype(o_ref.dtype)


def _mla_attn(qa, ka, va):
    tq = TQ
    nq = SEQ // tq
    return pl.pallas_call(
        _mla_attn_kernel,
        grid=(BATCH, MLA_HEADS, nq),
        in_specs=[pl.BlockSpec((tq, MLA_QK_PAD), lambda b, h, i: (b * nq + i, h)),
                  pl.BlockSpec((SEQ, MLA_QK_PAD), lambda b, h, i: (b, h)),
                  pl.BlockSpec((SEQ, MLA_V), lambda b, h, i: (b, h))],
        out_specs=pl.BlockSpec((tq, MLA_V), lambda b, h, i: (b * nq + i, h)),
        out_shape=jax.ShapeDtypeStruct((TOKENS, MLA_HEADS * MLA_V), BF16),
        scratch_shapes=_attn_scratch(1),
        compiler_params=_cparams(("parallel", "parallel", "arbitrary"), V7X_VMEM_LIMIT),
        name="mla_attn",
    )(qa, ka, va)


def _diff_attn_kernel(q_ref, k_ref, v_ref, lam_ref, g_ref, o_ref, vaug_s, m1, a1, m2, a2, *, lam_init):
    qi = pl.program_id(2)
    states = ((m1, a1), (m2, a2))
    _init_attn_state(v_ref, vaug_s, states)
    q = q_ref[...]
    lane = lax.broadcasted_iota(jnp.int32, q.shape, 1)
    zero = jnp.zeros_like(q)
    q1 = jnp.where(lane < DIFF_HEAD_DIM, q, zero)
    q2 = jnp.where(lane >= DIFF_HEAD_DIM, q, zero)

    def scores(j, row0):
        k = k_ref[_key_rows(j), :]
        if row0 is None:
            return (_nt_dot(q1, k), _nt_dot(q2, k))
        return (_diag_masked(_nt_dot(q1[row0:, :], k)), _diag_masked(_nt_dot(q2[row0:, :], k)))

    _attn_sweep(qi, scores, vaug_s, states)
    lp = lam_ref[...]
    lam = (jnp.exp(jnp.sum(lp[0:1] * lp[1:2], axis=1, keepdims=True))
           - jnp.exp(jnp.sum(lp[2:3] * lp[3:4], axis=1, keepdims=True)) + lam_init)
    od = _normalised(a1) - lam * _normalised(a2)
    o_ref[...] = (_rms(od) * g_ref[...] * (1.0 - lam_init)).astype(o_ref.dtype)


def _diff_attn(proj, diff_lambda, subln_g, layer):
    tq = TQ
    nq = SEQ // tq
    hd = 2 * DIFF_HEAD_DIM
    q0, k0, v0 = 1024 // hd, 2048 // hd, 3072 // hd
    lam_init = 0.8 - 0.6 * math.exp(-0.3 * layer)
    return pl.pallas_call(
        functools.partial(_diff_attn_kernel, lam_init=lam_init),
        grid=(BATCH, DIFF_HEADS, nq),
        in_specs=[pl.BlockSpec((tq, hd), lambda b, h, i: (b * nq + i, q0 + h)),
                  pl.BlockSpec((SEQ, hd), lambda b, h, i: (b, k0 + h)),
                  pl.BlockSpec((SEQ, hd), lambda b, h, i: (b, v0 + h)),
                  pl.BlockSpec((4, DIFF_HEAD_DIM), lambda b, h, i: (0, 0)),
                  pl.BlockSpec((1, hd), lambda b, h, i: (0, 0))],
        out_specs=pl.BlockSpec((tq, hd), lambda b, h, i: (b * nq + i, h)),
        out_shape=jax.ShapeDtypeStruct((TOKENS, DIFF_HEADS * hd), BF16),
        scratch_shapes=_attn_scratch(2),
        compiler_params=_cparams(("parallel", "parallel", "arbitrary"), V7X_VMEM_LIMIT),
        name="diff_attn",
    )(proj, proj, proj, diff_lambda, subln_g.reshape(1, hd))


def _post_residual(x, y, gate, g):
    return x + gate * (_rms(y) * g)


def _merge_out_kernel(ya_ref, yb_ref, ga_ref, gb_ref, bg_ref, wa_ref, wb_ref, wo_ref, x_ref, gate_ref, g_ref,
                      o_ref):
    a = jnp.dot(ya_ref[...], wa_ref[...], preferred_element_type=F32)
    b = jnp.dot(yb_ref[...], wb_ref[...], preferred_element_type=F32)
    bg = bg_ref[...]
    merged = (jax.nn.sigmoid(ga_ref[...].astype(F32) + bg[0:1]) * a
              + jax.nn.sigmoid(gb_ref[...].astype(F32) + bg[1:2]) * b)
    y = jnp.dot(merged.astype(BF16), wo_ref[...], preferred_element_type=F32)
    o_ref[...] = _post_residual(x_ref[...], y, gate_ref[0], g_ref[...])


def _resident(shape):
    return pl.BlockSpec(shape, lambda i: (0,) * len(shape), pipeline_mode=pl.Buffered(1))


def _merge_out(ya, yb, proj, b_gate, wa, wb, wo, x, gate, g):
    tm = TM_MERGE
    per_b = SEQ // tm
    ga_blk, gb_blk = 4096 // D_MODEL, 6144 // D_MODEL
    return pl.pallas_call(
        _merge_out_kernel,
        grid=(TOKENS // tm,),
        in_specs=[pl.BlockSpec((tm, 1024), lambda i: (i, 0)),
                  pl.BlockSpec((tm, 1024), lambda i: (i, 0)),
                  pl.BlockSpec((tm, D_MODEL), lambda i: (i, ga_blk)),
                  pl.BlockSpec((tm, D_MODEL), lambda i: (i, gb_blk)),
                  _resident((2, D_MODEL)),
                  _resident((1024, D_MODEL)),
                  _resident((1024, D_MODEL)),
                  _resident((D_MODEL, D_MODEL)),
                  pl.BlockSpec((tm, D_MODEL), lambda i: (i, 0)),
                  pl.BlockSpec((1, 1, D_MODEL), lambda i: (i // per_b, 0, 0)),
                  _resident((1, D_MODEL))],
        out_specs=pl.BlockSpec((tm, D_MODEL), lambda i: (i, 0)),
        out_shape=jax.ShapeDtypeStruct((TOKENS, D_MODEL), F32),
        compiler_params=_cparams(("parallel",), V7X_VMEM_LIMIT),
        name="merge_out",
    )(ya, yb, proj, proj, b_gate, wa, wb, wo, x, gate, g.reshape(1, D_MODEL))


def _new_weights(te_ref, m):
    return jnp.logical_or(m == 0, te_ref[m] != te_ref[jnp.maximum(m - 1, 0)])


def _grouped_kernel(te_ref, nv_ref, nu_ref, x_ref, *refs, tile_fn):
    n_w = (len(refs) - 1) // 2
    w_refs, o_ref, w_bf16 = refs[:n_w], refs[n_w], refs[n_w + 1:]
    m = pl.program_id(1)
    tm = x_ref.shape[0]
    in_use = m < nu_ref[0]

    @pl.when(jnp.logical_and(in_use, _new_weights(te_ref, m)))
    def _():
        for w_ref, w_s in zip(w_refs, w_bf16):
            w_s[...] = w_ref[0].astype(BF16)

    @pl.when(jnp.logical_and(in_use, nv_ref[m] > tm // 2))
    def _():
        o_ref[...] = tile_fn(x_ref[...], *w_bf16).astype(o_ref.dtype)

    @pl.when(jnp.logical_and(in_use, nv_ref[m] <= tm // 2))
    def _():
        o_ref[:tm // 2, :] = tile_fn(x_ref[:tm // 2, :], *w_bf16).astype(o_ref.dtype)
        o_ref[tm // 2:, :] = jnp.zeros((tm // 2, o_ref.shape[1]), o_ref.dtype)

    @pl.when(jnp.logical_not(in_use))
    def _():
        o_ref[...] = jnp.zeros(o_ref.shape, o_ref.dtype)


def _swiglu_tile(x, wg_s, wu_s):
    g = jnp.dot(x, wg_s[...], preferred_element_type=F32)
    u = jnp.dot(x, wu_s[...], preferred_element_type=F32)
    return g * jax.nn.sigmoid(g) * u


def _down_tile(x, w_s):
    return jnp.dot(x, w_s[...], preferred_element_type=F32)


def _grouped_matmul(tile_fn, x, w, tables, tm, tn, n_w, out_dtype, name):
    te, nv, nu = tables
    rows, k = x.shape
    n_tiles = w.shape[-1] // (n_w * tn)
    clamp = lambda m, nu_ref: jnp.minimum(m, nu_ref[0] - 1)
    w_specs = [pl.BlockSpec((1, k, tn),
                            lambda n, m, te_r, nv_r, nu_r, off=i * n_tiles: (te_r[clamp(m, nu_r)], 0, n + off))
               for i in range(n_w)]
    return pl.pallas_call(
        functools.partial(_grouped_kernel, tile_fn=tile_fn),
        grid_spec=pltpu.PrefetchScalarGridSpec(
            num_scalar_prefetch=3,
            grid=(n_tiles, rows // tm),
            in_specs=[pl.BlockSpec((tm, k), lambda n, m, te_r, nv_r, nu_r: (clamp(m, nu_r), 0))] + w_specs,
            out_specs=pl.BlockSpec((tm, tn), lambda n, m, te_r, nv_r, nu_r: (m, n)),
            scratch_shapes=[pltpu.VMEM((k, tn), BF16)] * n_w),
        out_shape=jax.ShapeDtypeStruct((rows, n_tiles * tn), out_dtype),
        compiler_params=_cparams(("arbitrary", "arbitrary"), V7X_VMEM_LIMIT_GROUPED),
        name=name,
    )(te, nv, nu, x, *([w] * n_w))


def _grouped_gu(x, w_gu, tables, tm, tn):
    return _grouped_matmul(_swiglu_tile, x, w_gu, tables, tm, tn, 2, BF16, "grouped_gate_up")


def _grouped_down(x, w_down, tables, tm, tn):
    return _grouped_matmul(_down_tile, x, w_down, tables, tm, tn, 1, F32, "grouped_down")


def _post_kernel(y_ref, x_ref, gate_ref, g_ref, o_ref):
    o_ref[...] = _post_residual(x_ref[...], y_ref[...], gate_ref[0], g_ref[...])


def _post(y, x, gate, g):
    tm = TM_NORM
    per_b = SEQ // tm
    return pl.pallas_call(
        _post_kernel,
        grid=(TOKENS // tm,),
        in_specs=[pl.BlockSpec((tm, D_MODEL), lambda i: (i, 0)),
                  pl.BlockSpec((tm, D_MODEL), lambda i: (i, 0)),
                  pl.BlockSpec((1, 1, D_MODEL), lambda i: (i // per_b, 0, 0)),
                  pl.BlockSpec((1, D_MODEL), lambda i: (0, 0))],
        out_specs=pl.BlockSpec((tm, D_MODEL), lambda i: (i, 0)),
        out_shape=jax.ShapeDtypeStruct((TOKENS, D_MODEL), F32),
        compiler_params=_cparams(("parallel",), V7X_VMEM_LIMIT),
        name="post_norm_residual",
    )(y, x, gate, g.reshape(1, D_MODEL))


def _row_copy(src_hbm, row, dst, slot, sem):
    return pltpu.make_async_copy(src_hbm.at[pl.ds(row, 1)], dst.at[pl.ds(slot, 1)], sem)


def _dispatch_kernel(src_ref, nv_ref, h_hbm, o_ref, buf, sem):
    i = pl.program_id(0)
    tm = buf.shape[0]

    @pl.when(nv_ref[i] > 0)
    def _():
        base = i * tm

        def issue(pair, c):
            for q in range(N_DMA_QUEUES):
                r = pair * N_DMA_QUEUES + q
                _row_copy(h_hbm, src_ref[base + r], buf, r, sem).start(priority=q)
            return c

        def wait(r, c):
            _row_copy(h_hbm, 0, buf, r, sem).wait()
            return c

        lax.fori_loop(0, tm // N_DMA_QUEUES, issue, 0)
        lax.fori_loop(0, tm, wait, 0)
        o_ref[...] = buf[...].astype(o_ref.dtype)

    @pl.when(nv_ref[i] == 0)
    def _():
        o_ref[...] = jnp.zeros(o_ref.shape, o_ref.dtype)


def _dispatch(h, src, nv):
    tm = TM_DISPATCH
    return pl.pallas_call(
        _dispatch_kernel,
        grid_spec=pltpu.PrefetchScalarGridSpec(
            num_scalar_prefetch=2,
            grid=(N_SORT_PAD // tm,),
            in_specs=[pl.BlockSpec(memory_space=pl.ANY)],
            out_specs=pl.BlockSpec((tm, D_MODEL), lambda i, s_r, nv_r: (i, 0)),
            scratch_shapes=[pltpu.VMEM((tm, D_MODEL), F32), pltpu.SemaphoreType.DMA(())]),
        out_shape=jax.ShapeDtypeStruct((N_SORT_PAD, D_MODEL), BF16),
        compiler_params=_cparams(("arbitrary",)),
        name="moe_dispatch",
    )(src, nv, h)


def _combine_kernel(p0_ref, p1_ref, ys_hbm, wt_ref, x_ref, gate_ref, g_ref, o_ref, buf0, buf1, sem):
    i = pl.program_id(0)
    tm = buf0.shape[0]
    base = i * tm

    def issue(r, c):
        _row_copy(ys_hbm, p0_ref[base + r], buf0, r, sem).start(priority=0)
        _row_copy(ys_hbm, p1_ref[base + r], buf1, r, sem).start(priority=1)
        return c

    def wait(r, c):
        _row_copy(ys_hbm, 0, buf0, r, sem).wait()
        _row_copy(ys_hbm, 0, buf1, r, sem).wait()
        return c

    lax.fori_loop(0, tm, issue, 0)
    lax.fori_loop(0, tm, wait, 0)
    wt = wt_ref[...]
    y = wt[:, 0:1] * buf0[...] + wt[:, 1:2] * buf1[...]
    o_ref[...] = _post_residual(x_ref[...], y, gate_ref[0], g_ref[...])


def _combine(ys, pos0, pos1, wt, x, gate, g):
    tm = TM_COMBINE
    per_b = SEQ // tm
    return pl.pallas_call(
        _combine_kernel,
        grid_spec=pltpu.PrefetchScalarGridSpec(
            num_scalar_prefetch=2,
            grid=(TOKENS // tm,),
            in_specs=[pl.BlockSpec(memory_space=pl.ANY),
                      pl.BlockSpec((tm, TOP_K), lambda i, a, b: (i, 0)),
                      pl.BlockSpec((tm, D_MODEL), lambda i, a, b: (i, 0)),
                      pl.BlockSpec((1, 1, D_MODEL), lambda i, a, b: (i // per_b, 0, 0)),
                      pl.BlockSpec((1, D_MODEL), lambda i, a, b: (0, 0))],
            out_specs=pl.BlockSpec((tm, D_MODEL), lambda i, a, b: (i, 0)),
            scratch_shapes=[pltpu.VMEM((tm, D_MODEL), F32), pltpu.VMEM((tm, D_MODEL), F32),
                            pltpu.SemaphoreType.DMA(())]),
        out_shape=jax.ShapeDtypeStruct((TOKENS, D_MODEL), F32),
        compiler_params=_cparams(("arbitrary",)),
        name="moe_combine",
    )(pos0, pos1, ys, wt, x, gate, g.reshape(1, D_MODEL))


def _routing_tables(top_i, rank, counts):
    tm = TM_MOE
    n_tiles = N_SORT_PAD // tm
    experts = jnp.arange(N_EXPERTS, dtype=jnp.int32)
    padded = ((counts + tm - 1) // tm) * tm
    ends = jnp.cumsum(padded)
    offs = ends - padded
    group_start = jnp.sum((top_i[:, :, None] == experts) * offs, axis=-1)
    pos = (group_start + rank).astype(jnp.int32)
    tok = jnp.tile(jnp.arange(TOKENS, dtype=jnp.int32), TOP_K)
    src = jnp.zeros((N_SORT_PAD,), jnp.int32).at[pos.reshape(-1)].set(tok)
    tile_start = jnp.arange(n_tiles, dtype=jnp.int32) * tm
    te = jnp.sum((tile_start[:, None] >= ends[None, :]).astype(jnp.int32), axis=1)
    te = jnp.minimum(te, N_EXPERTS - 1)
    nu = (ends[-1] // tm).astype(jnp.int32).reshape(1)
    nv = jnp.clip((offs + counts)[te] - tile_start, 0, tm).astype(jnp.int32)
    d_start = jnp.arange(N_SORT_PAD // TM_DISPATCH, dtype=jnp.int32) * TM_DISPATCH
    d_te = jnp.repeat(te, tm // TM_DISPATCH)
    nv_dispatch = jnp.clip((offs + counts)[d_te] - d_start, 0, TM_DISPATCH).astype(jnp.int32)
    return src, (te, nv, nu), nv_dispatch, pos[0], pos[1]


def _split_w_in(w):
    lat = MLA_Q_RANK + MLA_KV_RANK + MLA_ROPE
    w_lat = jnp.zeros((D_MODEL, TN_IN), BF16).at[:, :lat].set(w[:, :lat].astype(BF16))
    return w_lat, w[:, lat:].astype(BF16)


def _pad_w_q_up(w):
    w = w.reshape(MLA_Q_RANK, MLA_HEADS, MLA_NOPE + MLA_ROPE)
    pad = jnp.zeros((MLA_Q_RANK, MLA_HEADS, MLA_QK_PAD - MLA_NOPE - MLA_ROPE), w.dtype)
    return jnp.concatenate([w, pad], axis=2).reshape(MLA_Q_RANK, MLA_HEADS * MLA_QK_PAD).astype(BF16)


def _perm_w_kv_up(w):
    w = w.reshape(MLA_KV_RANK, MLA_HEADS, MLA_NOPE + MLA_V)
    k = w[:, :, :MLA_NOPE].reshape(MLA_KV_RANK, MLA_HEADS * MLA_NOPE)
    v = w[:, :, MLA_NOPE:].reshape(MLA_KV_RANK, MLA_HEADS * MLA_V)
    return jnp.concatenate([k, v], axis=1).astype(BF16)


def _one_group(tm):
    n_tiles = TOKENS // tm
    return jnp.zeros((n_tiles,), jnp.int32), jnp.full((n_tiles,), tm, jnp.int32), jnp.full((1,), n_tiles, jnp.int32)


def kernel(x, c, positions, ada_w, ada_b, norm_g, w_in, q_norm_g, kv_norm_g, w_q_up, w_kv_up, diff_lambda,
           diff_subln_g, w_branch_a, w_branch_b, b_gate, w_out, dense_w_gu, dense_w_down, router_w, moe_w_gu,
           moe_w_down):
    assert x.shape == (BATCH, SEQ, D_MODEL) and c.shape == (BATCH, D_MODEL)
    assert w_in.shape == (DEPTH, D_MODEL, 8000)
    xt = x.reshape(TOKENS, D_MODEL)
    cos, sin = _rope_tables(positions)
    mod = _adaln(c, ada_w, ada_b)

    for l in range(DEPTH):
        shift, scale, gate = (mod[l, 0, :, :, i * D_MODEL:(i + 1) * D_MODEL] for i in range(3))
        h = _norm_mod(xt, norm_g[l, 0], scale, shift)
        proj = _in_proj(h, *_split_w_in(w_in[l]), cos, sin)
        qa, ka, va = _mla_up(proj, q_norm_g[l], kv_norm_g[l], _pad_w_q_up(w_q_up[l]), _perm_w_kv_up(w_kv_up[l]),
                             cos, sin)
        ya = _mla_attn(qa, ka, va)
        yb = _diff_attn(proj, diff_lambda[l], diff_subln_g[l], l)
        xt = _merge_out(ya, yb, proj, b_gate[l], w_branch_a[l].astype(BF16), w_branch_b[l].astype(BF16),
                        w_out[l].astype(BF16), xt, gate, norm_g[l, 1])

        shift, scale, gate = (mod[l, 1, :, :, i * D_MODEL:(i + 1) * D_MODEL] for i in range(3))
        if l % 2 == 0:
            h = _norm_mod(xt, norm_g[l, 2], scale, shift)
            act = _grouped_gu(h, dense_w_gu[l // 2][None], _one_group(TM_DENSE_GU), TM_DENSE_GU, TN_DENSE_GU)
            y = _grouped_down(act, dense_w_down[l // 2][None], _one_group(TM_DENSE_DOWN), TM_DENSE_DOWN,
                              TN_DENSE_DOWN)
            xt = _post(y, xt, gate, norm_g[l, 3])
        else:
            h, top_i, top_w, rank, cnt = _norm_mod_route(xt, norm_g[l, 2], scale, shift, router_w[l // 2])
            src, tables, nv_dispatch, pos0, pos1 = _routing_tables(top_i, rank, cnt[:, 0].astype(jnp.int32))
            xs = _dispatch(h, src, nv_dispatch)
            act = _grouped_gu(xs, moe_w_gu[l // 2], tables, TM_MOE, TN_MOE_GU)
            ys = _grouped_down(act, moe_w_down[l // 2], tables, TM_MOE, TN_MOE_DOWN)
            xt = _combine(ys, pos0, pos1, top_w.T, xt, gate, norm_g[l, 3])
    return xt.reshape(BATCH, SEQ, D_MODEL)
```
